```python
import jax, jax.numpy as jnp
from jax import lax
import numpy as np

D_MODEL = 2048
BATCH = 2
SEQ = 16384
DEPTH = 1

GRID_W = 64
CTX_LEN = 256

GLA_HEADS = 4
GLA_DK = 128
GLA_DV = 256
GLA_KEY_WIDTH = GLA_HEADS * GLA_DK
GLA_WIDTH = GLA_HEADS * GLA_DV
GLA_GATE_RANK = 16
GLA_GATE_TAU = 16.0
GLA_CHUNK = 64

CMLP_GROUPS = 8
CMLP_GROUP_DIM = 128
CMLP_WIDTH = CMLP_GROUPS * CMLP_GROUP_DIM
CMLP_CHUNK = 128
ROWS_PER_CHUNK = CMLP_CHUNK // GRID_W

MIX_WIDTH = GLA_WIDTH + CMLP_WIDTH

IN_SIZES = (GLA_KEY_WIDTH, GLA_KEY_WIDTH, GLA_WIDTH, GLA_WIDTH, 2 * GLA_GATE_RANK, CMLP_WIDTH, CMLP_WIDTH)
IN_WIDTH = sum(IN_SIZES)
IN_OFFSETS = tuple(int(o) for o in np.cumsum(IN_SIZES)[:-1])

PEER_HEADS = 8
PEER_NKEYS = 128
PEER_EXPERTS = PEER_NKEYS * PEER_NKEYS
PEER_QDIM = 256
PEER_HALF = PEER_QDIM // 2
PEER_TOPK = 16
PEER_BLOCK = 128

N_MOD = 6
EPS = 1e-6

kernel_name = "hybrid_gla_chunkmlp_peer_dit_block"


def rmsnorm(x, g):
    xf = x.astype(jnp.float32)
    y = xf * lax.rsqrt(jnp.mean(xf * xf, axis=-1, keepdims=True) + EPS)
    return (y * g.astype(jnp.float32)).astype(x.dtype)


def modulate(x, shift, scale):
    return x * (1.0 + scale) + shift


def gla_scan(q, k, v, log_a, s0):
    b, l, h, _ = q.shape
    n = l // GLA_CHUNK

    def to_chunks(t):
        return t.astype(jnp.float32).reshape(b, n, GLA_CHUNK, h, t.shape[-1]).transpose(1, 0, 3, 2, 4)

    prefix_mask = jnp.tril(jnp.ones((GLA_CHUNK, GLA_CHUNK), dtype=bool))

    def step(state, inp):
        qc, kc, vc, ac = inp
        cum = jnp.cumsum(ac, axis=2)
        tot = cum[:, :, -1:, :]
        q_dec = qc * jnp.exp(cum)
        scores = jnp.einsum('bhid,bhjd->bhij', q_dec, kc * jnp.exp(-cum))
        scores = jnp.where(prefix_mask, scores, 0.0)
        out = (jnp.einsum('bhij,bhje->bhie', scores, vc)
               + jnp.einsum('bhid,bhde->bhie', q_dec, state))
        k_dec = kc * jnp.exp(tot - cum)
        state = (jnp.exp(tot[:, :, 0, :])[..., None] * state
                 + jnp.einsum('bhjd,bhje->bhde', k_dec, vc))
        return state, out

    state, out = lax.scan(step, s0, (to_chunks(q), to_chunks(k), to_chunks(v), to_chunks(log_a)))
    out = out.transpose(1, 0, 3, 2, 4).reshape(b, l, h, v.shape[-1])
    return out, state


def gla_heads(parts, w_gate_up, b_gate):
    q, k, v, g, lr = parts
    b, l, _ = q.shape
    q = q.reshape(b, l, GLA_HEADS, GLA_DK) * (GLA_DK ** -0.5)
    k = k.reshape(b, l, GLA_HEADS, GLA_DK)
    v = v.reshape(b, l, GLA_HEADS, GLA_DV)

    def log_decay(d):
        z = lr[..., d * GLA_GATE_RANK:(d + 1) * GLA_GATE_RANK] @ w_gate_up[d] + b_gate[d]
        return (jax.nn.log_sigmoid(z.astype(jnp.float32)) / GLA_GATE_TAU).reshape(b, l, GLA_HEADS, GLA_DK)

    return q, k, v, log_decay(0), log_decay(1), g


def gla_out(o, g, norm_g):
    b, l = o.shape[:2]
    o = o * lax.rsqrt(jnp.mean(o * o, axis=-1, keepdims=True) + EPS)
    o = o.reshape(b, l, GLA_WIDTH) * norm_g.astype(jnp.float32)
    return (o * jax.nn.silu(g.astype(jnp.float32))).astype(g.dtype)


def gla_mixer(lat_parts, ctx_parts, w_gate_up, b_gate, norm_g, need_ctx):
    ql, kl, vl, afl, abl, gl = gla_heads(lat_parts, w_gate_up, b_gate)
    qc, kc, vc, afc, abc, gc = gla_heads(ctx_parts, w_gate_up, b_gate)
    flip = lambda t: t[:, ::-1]
    s0 = jnp.zeros((qc.shape[0], GLA_HEADS, GLA_DK, GLA_DV), jnp.float32)
    o_cf, s_f = gla_scan(qc, kc, vc, afc, s0)
    o_cb, s_b = gla_scan(flip(qc), flip(kc), flip(vc), flip(abc), s0)
    o_lf, _ = gla_scan(ql, kl, vl, afl, s_f)
    o_lb, _ = gla_scan(flip(ql), flip(kl), flip(vl), flip(abl), s_b)
    y_lat = gla_out(o_lf + flip(o_lb), gl, norm_g)
    y_ctx = gla_out(o_cf + flip(o_cb), gc, norm_g) if need_ctx else None
    return y_lat, y_ctx


def chunk_mlp(u, v, ln_g, ln_b, w_s, b_s, n_chunks):
    b, l, _ = u.shape
    u = jax.nn.gelu(u, approximate=False)
    vf = jax.nn.gelu(v.astype(jnp.float32), approximate=False)
    mu = jnp.mean(vf, axis=-1, keepdims=True)
    var = jnp.mean(jnp.square(vf - mu), axis=-1, keepdims=True)
    vn = ((vf - mu) * lax.rsqrt(var + EPS) * ln_g.astype(jnp.float32) + ln_b.astype(jnp.float32)).astype(u.dtype)
    vn = vn.reshape(b, n_chunks, CMLP_CHUNK, CMLP_GROUPS, CMLP_GROUP_DIM)
    s = jnp.einsum('gpq,bnqgc->bnpgc', w_s, vn) + b_s.T[:, :, None]
    return u * s.reshape(b, l, CMLP_WIDTH)


def peer_ffn(xn, w_q, sub_keys, exp_u, exp_v):
    b, l, d = xn.shape
    blocks = xn.reshape(-1, PEER_BLOCK, d)

    def block(xb):
        p = xb.shape[0]
        q = (xb @ w_q).reshape(p, PEER_HEADS, 2, PEER_HALF)
        s1 = jnp.einsum('phd,hkd->phk', q[:, :, 0], sub_keys[0])
        s2 = jnp.einsum('phd,hkd->phk', q[:, :, 1], sub_keys[1])
        v1, i1 = lax.top_k(s1, PEER_TOPK)
        v2, i2 = lax.top_k(s2, PEER_TOPK)
        cand = (v1[..., :, None] + v2[..., None, :]).reshape(p, PEER_HEADS, PEER_TOPK * PEER_TOPK)
        cidx = (i1[..., :, None] * PEER_NKEYS + i2[..., None, :]).reshape(p, PEER_HEADS, PEER_TOPK * PEER_TOPK)
        best, pos = lax.top_k(cand, PEER_TOPK)
        eidx = jnp.take_along_axis(cidx, pos, axis=-1)
        gate = jax.nn.softmax(best.astype(jnp.float32), axis=-1).astype(xb.dtype)
        u = exp_u[eidx]
        act = jax.nn.gelu(jnp.einsum('pd,phkd->phk', xb, u), approximate=False)
        return jnp.einsum('phk,phkd->pd', gate * act, exp_v[eidx])

    return lax.map(block, blocks).reshape(b, l, d)


def setup_inputs(seed: int = 0) -> dict:
    key = jax.random.key(seed)
    ks = jax.random.split(key, 24)
    f32 = jnp.float32
    nrm = lambda k, shape, s: jax.random.normal(k, shape, f32) * s
    D = D_MODEL
    return {
        "x": nrm(ks[0], (BATCH, SEQ, D), 1.0),
        "c": nrm(ks[1], (BATCH, D), 1.0),
        "ctx": nrm(ks[2], (BATCH, CTX_LEN, D), 1.0),
        "c_ctx": nrm(ks[3], (D,), 1.0),
        "norm1_g": 1.0 + nrm(ks[4], (DEPTH, D), 0.02),
        "norm2_g": 1.0 + nrm(ks[5], (DEPTH, D), 0.02),
        "w_mod": nrm(ks[6], (DEPTH, D, N_MOD * D), D ** -0.5),
        "b_mod": nrm(ks[7], (DEPTH, N_MOD * D), 0.02),
        "w_in": nrm(ks[8], (DEPTH, D, IN_WIDTH), D ** -0.5),
        "w_gate_up": nrm(ks[9], (DEPTH, 2, GLA_GATE_RANK, GLA_KEY_WIDTH), GLA_GATE_RANK ** -0.5),
        "b_gate": nrm(ks[10], (DEPTH, 2, GLA_KEY_WIDTH), 0.1),
        "gla_norm_g": 1.0 + nrm(ks[11], (DEPTH, GLA_WIDTH), 0.02),
        "cmlp_ln_g": 1.0 + nrm(ks[12], (DEPTH, CMLP_WIDTH), 0.02),
        "cmlp_ln_b": nrm(ks[13], (DEPTH, CMLP_WIDTH), 0.02),
        "w_spatial": nrm(ks[14], (DEPTH, CMLP_GROUPS, CMLP_CHUNK, CMLP_CHUNK), CMLP_CHUNK ** -0.5),
        "b_spatial": nrm(ks[15], (DEPTH, CMLP_GROUPS, CMLP_CHUNK), 0.02),
        "w_out": nrm(ks[16], (DEPTH, MIX_WIDTH, D), MIX_WIDTH ** -0.5),
        "peer_wq": nrm(ks[17], (DEPTH, D, PEER_HEADS * PEER_QDIM), D ** -0.5),
        "peer_sub_keys": nrm(ks[18], (DEPTH, 2, PEER_HEADS, PEER_NKEYS, PEER_HALF), PEER_HALF ** -0.5),
        "peer_u": nrm(ks[19], (DEPTH, PEER_EXPERTS, D), D ** -0.5),
        "peer_v": nrm(ks[20], (DEPTH, PEER_EXPERTS, D), PEER_HEADS ** -0.5),
        "final_norm_g": 1.0 + nrm(ks[21], (D,), 0.02),
    }


def reference(x, c, ctx, c_ctx, norm1_g, norm2_g, w_mod, b_mod, w_in, w_gate_up, b_gate,
              gla_norm_g, cmlp_ln_g, cmlp_ln_b, w_spatial, b_spatial, w_out,
              peer_wq, peer_sub_keys, peer_u, peer_v, final_norm_g):
    rows = x.shape[1] // GRID_W
    lat_chunks = rows // ROWS_PER_CHUNK
    ctx_chunks = ctx.shape[1] // CMLP_CHUNK
    silu_c = jax.nn.silu(c)
    silu_cc = jax.nn.silu(c_ctx)
    h, hc = x, ctx
    for i in range(DEPTH):
        need_ctx = i + 1 < DEPTH
        m = silu_c @ w_mod[i] + b_mod[i]
        mc = silu_cc @ w_mod[i] + b_mod[i]
        sh1, sc1, g1, sh2, sc2, g2 = jnp.split(m[:, None, :], N_MOD, axis=-1)
        csh1, csc1, cg1, csh2, csc2, cg2 = jnp.split(mc, N_MOD, axis=-1)

        p_lat = jnp.split(modulate(rmsnorm(h, norm1_g[i]), sh1, sc1) @ w_in[i], IN_OFFSETS, axis=-1)
        p_ctx = jnp.split(modulate(rmsnorm(hc, norm1_g[i]), csh1, csc1) @ w_in[i], IN_OFFSETS, axis=-1)
        gla_lat, gla_ctx = gla_mixer(p_lat[:5], p_ctx[:5], w_gate_up[i], b_gate[i], gla_norm_g[i], need_ctx)
        cm_lat = chunk_mlp(p_lat[5], p_lat[6], cmlp_ln_g[i], cmlp_ln_b[i], w_spatial[i], b_spatial[i], lat_chunks)
        h = h + g1 * (jnp.concatenate([gla_lat, cm_lat], axis=-1) @ w_out[i])

        h = h + g2 * peer_ffn(modulate(rmsnorm(h, norm2_g[i]), sh2, sc2),
                              peer_wq[i], peer_sub_keys[i], peer_u[i], peer_v[i])

        if need_ctx:
            cm_ctx = chunk_mlp(p_ctx[5], p_ctx[6], cmlp_ln_g[i], cmlp_ln_b[i], w_spatial[i], b_spatial[i], ctx_chunks)
            hc = hc + cg1 * (jnp.concatenate([gla_ctx, cm_ctx], axis=-1) @ w_out[i])
            hc = hc + cg2 * peer_ffn(modulate(rmsnorm(hc, norm2_g[i]), csh2, csc2),
                                     peer_wq[i], peer_sub_keys[i], peer_u[i], peer_v[i])
    return rmsnorm(h, final_norm_g)
```

```python
import functools

import jax
import jax.numpy as jnp
from jax import lax
from jax.experimental import pallas as pl
from jax.experimental.pallas import tpu as pltpu

F32 = jnp.float32
BF16 = jnp.bfloat16

D_MODEL = 2048
GLA_HEADS = 4
GLA_DK = 128
GLA_DV = 256
GLA_KEY_WIDTH = GLA_HEADS * GLA_DK
GLA_WIDTH = GLA_HEADS * GLA_DV
GLA_GATE_RANK = 16
GLA_GATE_TAU = 16.0
GLA_CHUNK = 64
CMLP_GROUPS = 8
CMLP_GROUP_DIM = 128
CMLP_WIDTH = CMLP_GROUPS * CMLP_GROUP_DIM
CMLP_CHUNK = 128
PEER_HEADS = 8
PEER_NKEYS = 128
PEER_EXPERTS = PEER_NKEYS * PEER_NKEYS
PEER_QDIM = 256
PEER_HALF = PEER_QDIM // 2
PEER_TOPK = 16
N_MOD = 6
EPS = 1e-6

MAIN_WIDTH = 2 * GLA_KEY_WIDTH + 2 * GLA_WIDTH + 2 * CMLP_WIDTH
LR_PAD = 128
MOD_ROWS = 8

VMEM_LIMIT_BYTES = 56 * 1024 * 1024

_NT = (((1,), (1,)), ((), ()))
_TN = (((0,), (0,)), ((), ()))


def _params(*sem):
    return pltpu.CompilerParams(dimension_semantics=sem, vmem_limit_bytes=VMEM_LIMIT_BYTES)


def _gelu(x):
    return 0.5 * x * (1.0 + lax.erf(x * (2.0 ** -0.5)))


def _silu(x):
    return x * jax.nn.sigmoid(x)


def _mod_kernel(cc_ref, w_ref, b_ref, o_ref):
    s = _silu(cc_ref[...])
    o_ref[...] = jnp.dot(s, w_ref[...], precision=lax.Precision.HIGHEST,
                         preferred_element_type=F32) + b_ref[...]


def _modulation(cc, w_mod, b_mod):
    n = w_mod.shape[1]
    tn = 1024
    return pl.pallas_call(
        _mod_kernel,
        grid=(n // tn,),
        in_specs=[pl.BlockSpec((MOD_ROWS, D_MODEL), lambda j: (0, 0)),
                  pl.BlockSpec((D_MODEL, tn), lambda j: (0, j)),
                  pl.BlockSpec((1, tn), lambda j: (0, j))],
        out_specs=pl.BlockSpec((MOD_ROWS, tn), lambda j: (0, j)),
        out_shape=jax.ShapeDtypeStruct((MOD_ROWS, n), F32),
        compiler_params=_params("arbitrary"),
        name="mod",
    )(cc, w_mod, b_mod.reshape(1, n))


def _inproj_kernel(x_ref, mod_ref, g_ref, w_ref, wlr_ref, p_ref, lr_ref, xn_ref):
    @pl.when(pl.program_id(1) == 0)
    def _():
        x = x_ref[...]
        y = x * lax.rsqrt(jnp.mean(x * x, axis=-1, keepdims=True) + EPS) * g_ref[...]
        xn = (y * (1.0 + mod_ref[0, 1:2, :]) + mod_ref[0, 0:1, :]).astype(BF16)
        xn_ref[...] = xn
        lr_ref[...] = jnp.dot(xn, wlr_ref[...], preferred_element_type=F32).astype(lr_ref.dtype)

    p_ref[...] = jnp.dot(xn_ref[...], w_ref[...], preferred_element_type=F32).astype(p_ref.dtype)


def _inproj(x2, mod3, norm_g, w_main, w_lr, mod_row_of_tile):
    t = x2.shape[0]
    tm = min(512, t)
    tn = 1024
    return pl.pallas_call(
        _inproj_kernel,
        grid=(t // tm, MAIN_WIDTH // tn),
        in_specs=[pl.BlockSpec((tm, D_MODEL), lambda i, j: (i, 0)),
                  pl.BlockSpec((1, N_MOD, D_MODEL), lambda i, j: (mod_row_of_tile(i, tm), 0, 0)),
                  pl.BlockSpec((1, D_MODEL), lambda i, j: (0, 0)),
                  pl.BlockSpec((D_MODEL, tn), lambda i, j: (0, j)),
                  pl.BlockSpec((D_MODEL, LR_PAD), lambda i, j: (0, 0))],
        out_specs=[pl.BlockSpec((tm, tn), lambda i, j: (i, j)),
                   pl.BlockSpec((tm, LR_PAD), lambda i, j: (i, 0))],
        out_shape=[jax.ShapeDtypeStruct((t, MAIN_WIDTH), BF16),
                   jax.ShapeDtypeStruct((t, LR_PAD), BF16)],
        scratch_shapes=[pltpu.VMEM((tm, D_MODEL), BF16)],
        compiler_params=_params("arbitrary", "arbitrary"),
        name="inproj",
    )(x2, mod3, norm_g, w_main, w_lr)


def _gla_kernel(qf_ref, kf_ref, vf_ref, lrf_ref, qb_ref, kb_ref, vb_ref, lrb_ref,
                wup_ref, bg_ref, s0_ref, of_ref, ob_ref, st_ref, *, n_chunks):
    @pl.when(pl.program_id(1) == 0)
    def _():
        st_ref[...] = s0_ref[...]

    c = GLA_CHUNK
    row = lax.broadcasted_iota(jnp.int32, (c, c), 0)
    col = lax.broadcasted_iota(jnp.int32, (c, c), 1)
    keep = (col <= row, col >= row)
    q_scale = GLA_DK ** -0.5
    dirs = ((qf_ref, kf_ref, vf_ref, lrf_ref, of_ref), (qb_ref, kb_ref, vb_ref, lrb_ref, ob_ref))

    def chunk_step(ci, carry):
        for d, (q_ref, k_ref, v_ref, lr_ref, o_ref) in enumerate(dirs):
            cc = ci if d == 0 else n_chunks - 1 - ci
            r0 = pl.multiple_of(cc * c, c)
            rows = pl.ds(r0, c)
            z = jnp.dot(lr_ref[rows, :], wup_ref[d], preferred_element_type=F32) + bg_ref[d]
            la = (jnp.minimum(z, 0.0) - jnp.log1p(jnp.exp(-jnp.abs(z)))) * (1.0 / GLA_GATE_TAU)
            cum = jnp.dot(keep[d].astype(F32), la, precision=lax.Precision.HIGHEST,
                          preferred_element_type=F32)
            tot = cum[c - 1:c, :] if d == 0 else cum[0:1, :]
            qd = (q_ref[rows, :].astype(F32) * q_scale * jnp.exp(cum)).astype(BF16)
            kf = k_ref[rows, :].astype(F32)
            kd = (kf * jnp.exp(-cum)).astype(BF16)
            k2 = (kf * jnp.exp(tot - cum)).astype(BF16)
            etot = jnp.exp(tot)
            for h in range(GLA_HEADS):
                ks = slice(h * GLA_DK, (h + 1) * GLA_DK)
                vs = slice(h * GLA_DV, (h + 1) * GLA_DV)
                vh = v_ref[rows, vs]
                st = st_ref[0, d, h]
                sc = lax.dot_general(qd[:, ks], kd[:, ks], _NT, preferred_element_type=F32)
                sc = jnp.where(keep[d], sc, 0.0).astype(BF16)
                o = (jnp.dot(sc, vh, preferred_element_type=F32)
                     + lax.dot_general(qd[:, ks], st.astype(BF16), _NT, preferred_element_type=F32))
                o_ref[rows, vs] = o
                st_ref[0, d, h] = st * etot[:, ks] + lax.dot_general(
                    vh, k2[:, ks], _TN, preferred_element_type=F32)
        return carry

    lax.fori_loop(0, n_chunks, chunk_step, 0)


def _gla(p, lr, w_up, b_gate, s0, batch, seq):
    tb = min(512, seq)
    nb = seq // tb
    qw, vw = GLA_KEY_WIDTH, GLA_WIDTH
    fwd = lambda col: (lambda b, s: (b * nb + s, col))
    bwd = lambda col: (lambda b, s: (b * nb + nb - 1 - s, col))
    state_spec = pl.BlockSpec((1, 2, GLA_HEADS, GLA_DV, GLA_DK), lambda b, s: (b, 0, 0, 0, 0))
    in_specs = []
    for m in (fwd, bwd):
        in_specs += [pl.BlockSpec((tb, qw), m(0)), pl.BlockSpec((tb, qw), m(1)),
                     pl.BlockSpec((tb, vw), m(1)), pl.BlockSpec((tb, LR_PAD), m(0))]
    in_specs += [pl.BlockSpec((2, LR_PAD, qw), lambda b, s: (0, 0, 0)),
                 pl.BlockSpec((2, 1, qw), lambda b, s: (0, 0, 0)),
                 state_spec]
    return pl.pallas_call(
        functools.partial(_gla_kernel, n_chunks=tb // GLA_CHUNK),
        grid=(batch, nb),
        in_specs=in_specs,
        out_specs=[pl.BlockSpec((tb, vw), fwd(0)), pl.BlockSpec((tb, vw), bwd(0)), state_spec],
        out_shape=[jax.ShapeDtypeStruct((batch * seq, vw), F32),
                   jax.ShapeDtypeStruct((batch * seq, vw), F32),
                   jax.ShapeDtypeStruct((batch, 2, GLA_HEADS, GLA_DV, GLA_DK), F32)],
        compiler_params=_params("arbitrary", "arbitrary"),
        name="gla",
    )(p, p, p, lr, p, p, p, lr, w_up, b_gate, s0)


def _cmlp_kernel(u_ref, v_ref, lng_ref, lnb_ref, ws_ref, bs_ref, o_ref, *, n_chunks):
    gv = _gelu(v_ref[...].astype(F32))
    mu = jnp.mean(gv, axis=-1, keepdims=True)
    dv = gv - mu
    var = jnp.mean(dv * dv, axis=-1, keepdims=True)
    vn = (dv * lax.rsqrt(var + EPS) * lng_ref[...] + lnb_ref[...]).astype(BF16)
    for ch in range(n_chunks):
        rows = slice(ch * CMLP_CHUNK, (ch + 1) * CMLP_CHUNK)
        for g in range(CMLP_GROUPS):
            cols = slice(g * CMLP_GROUP_DIM, (g + 1) * CMLP_GROUP_DIM)
            s = jnp.dot(ws_ref[g], vn[rows, cols], preferred_element_type=F32) + bs_ref[:, g:g + 1]
            o_ref[rows, cols] = (_gelu(u_ref[rows, cols].astype(F32)) * s).astype(o_ref.dtype)


def _cmlp(p, ln_g, ln_b, w_s, b_st):
    t = p.shape[0]
    tm = min(512, t)
    cu_blk = (2 * GLA_KEY_WIDTH + 2 * GLA_WIDTH) // CMLP_WIDTH
    return pl.pallas_call(
        functools.partial(_cmlp_kernel, n_chunks=tm // CMLP_CHUNK),
        grid=(t // tm,),
        in_specs=[pl.BlockSpec((tm, CMLP_WIDTH), lambda i: (i, cu_blk)),
                  pl.BlockSpec((tm, CMLP_WIDTH), lambda i: (i, cu_blk + 1)),
                  pl.BlockSpec((1, CMLP_WIDTH), lambda i: (0, 0)),
                  pl.BlockSpec((1, CMLP_WIDTH), lambda i: (0, 0)),
                  pl.BlockSpec((CMLP_GROUPS, CMLP_CHUNK, CMLP_CHUNK), lambda i: (0, 0, 0)),
                  pl.BlockSpec((CMLP_CHUNK, CMLP_GROUPS), lambda i: (0, 0))],
        out_specs=pl.BlockSpec((tm, CMLP_WIDTH), lambda i: (i, 0)),
        out_shape=jax.ShapeDtypeStruct((t, CMLP_WIDTH), BF16),
        compiler_params=_params("arbitrary"),
        name="cmlp",
    )(p, p, ln_g, ln_b, w_s, b_st)


def _outproj_kernel(of_ref, ob_ref, g_ref, cm_ref, x_ref, mod_ref, gng_ref, n2g_ref, w_ref,
                    h1_ref, xn2_ref):
    o = of_ref[...] + ob_ref[...]
    heads = []
    for h in range(GLA_HEADS):
        oh = o[:, h * GLA_DV:(h + 1) * GLA_DV]
        heads.append(oh * lax.rsqrt(jnp.mean(oh * oh, axis=-1, keepdims=True) + EPS))
    on = jnp.concatenate(heads, axis=-1) * gng_ref[...]
    y = (on * _silu(g_ref[...].astype(F32))).astype(BF16)
    mix = (jnp.dot(y, w_ref[0:GLA_WIDTH, :], preferred_element_type=F32)
           + jnp.dot(cm_ref[...], w_ref[GLA_WIDTH:GLA_WIDTH + CMLP_WIDTH, :], preferred_element_type=F32))
    h1 = x_ref[...] + mod_ref[0, 2:3, :] * mix
    h1_ref[...] = h1
    hn = h1 * lax.rsqrt(jnp.mean(h1 * h1, axis=-1, keepdims=True) + EPS) * n2g_ref[...]
    xn2_ref[...] = (hn * (1.0 + mod_ref[0, 4:5, :]) + mod_ref[0, 3:4, :]).astype(xn2_ref.dtype)


def _outproj(o_f, o_b, p, cm, x2, mod3, gla_norm_g, norm2_g, w_out, seq):
    t = x2.shape[0]
    tm = min(512, t)
    g_blk = (2 * GLA_KEY_WIDTH + GLA_WIDTH) // GLA_WIDTH
    return pl.pallas_call(
        _outproj_kernel,
        grid=(t // tm,),
        in_specs=[pl.BlockSpec((tm, GLA_WIDTH), lambda i: (i, 0)),
                  pl.BlockSpec((tm, GLA_WIDTH), lambda i: (i, 0)),
                  pl.BlockSpec((tm, GLA_WIDTH), lambda i: (i, g_blk)),
                  pl.BlockSpec((tm, CMLP_WIDTH), lambda i: (i, 0)),
                  pl.BlockSpec((tm, D_MODEL), lambda i: (i, 0)),
                  pl.BlockSpec((1, N_MOD, D_MODEL), lambda i: ((i * tm) // seq, 0, 0)),
                  pl.BlockSpec((1, GLA_WIDTH), lambda i: (0, 0)),
                  pl.BlockSpec((1, D_MODEL), lambda i: (0, 0)),
                  pl.BlockSpec((GLA_WIDTH + CMLP_WIDTH, D_MODEL), lambda i: (0, 0))],
        out_specs=[pl.BlockSpec((tm, D_MODEL), lambda i: (i, 0)),
                   pl.BlockSpec((tm, D_MODEL), lambda i: (i, 0))],
        out_shape=[jax.ShapeDtypeStruct((t, D_MODEL), F32),
                   jax.ShapeDtypeStruct((t, D_MODEL), BF16)],
        compiler_params=_params("arbitrary"),
        name="outproj",
    )(o_f, o_b, p, cm, x2, mod3, gla_norm_g, norm2_g, w_out)


def _top16(s, key_iota):
    n = s.shape[1]
    neg = jnp.float32(-jnp.inf)
    slot = lax.broadcasted_iota(jnp.int32, (PEER_TOPK, n), 0)
    vals = jnp.zeros((PEER_TOPK, n), F32)
    rank = jnp.full(s.shape, PEER_TOPK, jnp.int32)
    for r in range(PEER_TOPK):
        m = jnp.max(s, axis=0, keepdims=True)
        idx = jnp.min(jnp.where(s == m, key_iota, s.shape[0]), axis=0, keepdims=True)
        sel = key_iota == idx
        rank = jnp.where(sel, r, rank)
        s = jnp.where(sel, neg, s)
        vals = jnp.where(slot == r, m, vals)
    return vals, rank


def _route_kernel(xn_ref, wqt_ref, sk_ref, e1_ref, c1_ref, e2_ref, r2_ref, qt_ref):
    qt_ref[...] = lax.dot_general(wqt_ref[...], xn_ref[...], _NT, preferred_element_type=F32)
    n = xn_ref.shape[0]
    k = PEER_TOPK
    key_iota = lax.broadcasted_iota(jnp.int32, (PEER_NKEYS, n), 0)
    cand_iota = lax.broadcasted_iota(jnp.int32, (k * k, n), 0)
    neg = jnp.float32(-jnp.inf)

    def head(h, carry):
        base = pl.multiple_of(h * PEER_QDIM, PEER_QDIM)
        q1 = qt_ref[pl.ds(base, PEER_HALF), :].astype(BF16)
        q2 = qt_ref[pl.ds(base + PEER_HALF, PEER_HALF), :].astype(BF16)
        s1 = jnp.dot(sk_ref[0, h], q1, preferred_element_type=F32)
        s2 = jnp.dot(sk_ref[1, h], q2, preferred_element_type=F32)
        v1, rank1 = _top16(s1, key_iota)
        v2, rank2 = _top16(s2, key_iota)
        cand = jnp.concatenate([v1[a:a + 1, :] + v2 for a in range(k)], axis=0)
        best = cand[0:1, :]
        rem = cand
        for _ in range(k):
            m = jnp.max(rem, axis=0, keepdims=True)
            idx = jnp.min(jnp.where(rem == m, cand_iota, k * k), axis=0, keepdims=True)
            rem = jnp.where(cand_iota == idx, neg, rem)
        chosen = rem != cand
        z = jnp.sum(jnp.where(chosen, jnp.exp(cand - best), 0.0), axis=0, keepdims=True)
        c1 = jnp.zeros((PEER_NKEYS, n), jnp.int32)
        for a in range(k):
            cnt = jnp.sum(chosen[a * k:(a + 1) * k, :].astype(jnp.int32), axis=0, keepdims=True)
            c1 = jnp.where(rank1 == a, cnt, c1)
        e1_ref[h] = jnp.exp(s1 - v1[0:1, :]) / z
        c1_ref[h] = c1.astype(F32)
        e2_ref[h] = jnp.exp(s2 - v2[0:1, :])
        r2_ref[h] = rank2.astype(F32)
        return carry

    lax.fori_loop(0, PEER_HEADS, head, 0)


def _route(xn2, wq_t, sk):
    t = xn2.shape[0]
    tr = min(256, t)
    tab = jax.ShapeDtypeStruct((PEER_HEADS, PEER_NKEYS, t), F32)
    tab_spec = pl.BlockSpec((PEER_HEADS, PEER_NKEYS, tr), lambda i: (0, 0, i))
    return pl.pallas_call(
        _route_kernel,
        grid=(t // tr,),
        in_specs=[pl.BlockSpec((tr, D_MODEL), lambda i: (i, 0)),
                  pl.BlockSpec((PEER_HEADS * PEER_QDIM, D_MODEL), lambda i: (0, 0)),
                  pl.BlockSpec((2, PEER_HEADS, PEER_NKEYS, PEER_HALF), lambda i: (0, 0, 0, 0))],
        out_specs=[tab_spec] * 4,
        out_shape=[tab] * 4,
        scratch_shapes=[pltpu.VMEM((PEER_HEADS * PEER_QDIM, tr), F32)],
        compiler_params=_params("arbitrary"),
        name="route",
    )(xn2, wq_t, sk)


def _peer_kernel(xn_ref, e1_ref, c1_ref, e2_ref, r2_ref, u_ref, vt_ref, h1_ref, mod_ref, fg_ref,
                 o_ref, acc_ref, p_ref, *, rows_per_step):
    j = pl.program_id(1)

    @pl.when(j == 0)
    def _():
        acc_ref[...] = jnp.zeros_like(acc_ref)

    n = xn_ref.shape[0]
    xn = xn_ref[...]
    for r in range(rows_per_step):
        i1 = j * rows_per_step + r
        erows = slice(r * PEER_NKEYS, (r + 1) * PEER_NKEYS)
        act = _gelu(lax.dot_general(u_ref[erows, :], xn, _NT, preferred_element_type=F32))
        w = jnp.zeros((PEER_NKEYS, n), F32)
        for h in range(PEER_HEADS):
            e1 = e1_ref[h, pl.ds(i1, 1), :]
            c1 = c1_ref[h, pl.ds(i1, 1), :]
            w = w + e1 * jnp.where(r2_ref[h] < c1, e2_ref[h], 0.0)
        p_ref[erows, :] = (w * act).astype(BF16)
    acc_ref[...] += jnp.dot(vt_ref[...], p_ref[...], preferred_element_type=F32)

    @pl.when(j == pl.num_programs(1) - 1)
    def _():
        h2 = h1_ref[...] + mod_ref[0, 5:6, :] * acc_ref[...].T
        o_ref[...] = h2 * lax.rsqrt(jnp.mean(h2 * h2, axis=-1, keepdims=True) + EPS) * fg_ref[...]


def _peer(xn2, tabs, u, v_t, h1, mod3, final_g, seq):
    t = xn2.shape[0]
    tb = min(512, t)
    ec = 512
    tab_spec = pl.BlockSpec((PEER_HEADS, PEER_NKEYS, tb), lambda i, j: (0, 0, i))
    return pl.pallas_call(
        functools.partial(_peer_kernel, rows_per_step=ec // PEER_NKEYS),
        grid=(t // tb, PEER_EXPERTS // ec),
        in_specs=[pl.BlockSpec((tb, D_MODEL), lambda i, j: (i, 0)),
                  tab_spec, tab_spec, tab_spec, tab_spec,
                  pl.BlockSpec((ec, D_MODEL), lambda i, j: (j, 0)),
                  pl.BlockSpec((D_MODEL, ec), lambda i, j: (0, j)),
                  pl.BlockSpec((tb, D_MODEL), lambda i, j: (i, 0)),
                  pl.BlockSpec((1, N_MOD, D_MODEL), lambda i, j: ((i * tb) // seq, 0, 0)),
                  pl.BlockSpec((1, D_MODEL), lambda i, j: (0, 0))],
        out_specs=pl.BlockSpec((tb, D_MODEL), lambda i, j: (i, 0)),
        out_shape=jax.ShapeDtypeStruct((t, D_MODEL), F32),
        scratch_shapes=[pltpu.VMEM((D_MODEL, tb), F32), pltpu.VMEM((ec, tb), BF16)],
        compiler_params=_params("arbitrary", "arbitrary"),
        name="peer",
    )(xn2, *tabs, u, v_t, h1, mod3, final_g)


def kernel(x, c, ctx, c_ctx, norm1_g, norm2_g, w_mod, b_mod, w_in, w_gate_up, b_gate, gla_norm_g,
           cmlp_ln_g, cmlp_ln_b, w_spatial, b_spatial, w_out, peer_wq, peer_sub_keys, peer_u, peer_v,
           final_norm_g):
    batch, seq, d = x.shape
    ctx_len = ctx.shape[1]
    depth = w_mod.shape[0]
    assert d == D_MODEL and depth == 1 and batch + 1 <= MOD_ROWS
    i = 0

    cc = jnp.zeros((MOD_ROWS, d), F32).at[:batch].set(c).at[batch].set(c_ctx)
    mod3 = _modulation(cc, w_mod[i], b_mod[i]).reshape(MOD_ROWS, N_MOD, d)

    qk_end = 2 * GLA_KEY_WIDTH + 2 * GLA_WIDTH
    lr_end = qk_end + 2 * GLA_GATE_RANK
    w = w_in[i]
    w_main = jnp.concatenate([w[:, :qk_end], w[:, lr_end:]], axis=1).astype(BF16)
    w_lr = jnp.pad(w[:, qk_end:lr_end], ((0, 0), (0, LR_PAD - 2 * GLA_GATE_RANK))).astype(BF16)
    n1 = norm1_g[i].reshape(1, d)
    x2 = x.reshape(batch * seq, d)
    p_lat, lr_lat = _inproj(x2, mod3, n1, w_main, w_lr, lambda t, tm: (t * tm) // seq)
    p_ctx, lr_ctx = _inproj(ctx.reshape(batch * ctx_len, d), mod3, n1, w_main, w_lr, lambda t, tm: batch)

    w_up = jnp.zeros((2, LR_PAD, GLA_KEY_WIDTH), F32)
    for dd in range(2):
        w_up = w_up.at[dd, dd * GLA_GATE_RANK:(dd + 1) * GLA_GATE_RANK].set(w_gate_up[i, dd])
    w_up = w_up.astype(BF16)
    bg = b_gate[i].reshape(2, 1, GLA_KEY_WIDTH)
    s_zero = jnp.zeros((batch, 2, GLA_HEADS, GLA_DV, GLA_DK), F32)
    _, _, s_ctx = _gla(p_ctx, lr_ctx, w_up, bg, s_zero, batch, ctx_len)
    o_f, o_b, _ = _gla(p_lat, lr_lat, w_up, bg, s_ctx, batch, seq)

    cm = _cmlp(p_lat, cmlp_ln_g[i].reshape(1, -1), cmlp_ln_b[i].reshape(1, -1),
               w_spatial[i].astype(BF16), b_spatial[i].T)

    h1, xn2 = _outproj(o_f, o_b, p_lat, cm, x2, mod3, gla_norm_g[i].reshape(1, -1),
                       norm2_g[i].reshape(1, d), w_out[i].astype(BF16), seq)

    tabs = _route(xn2, peer_wq[i].T.astype(BF16), peer_sub_keys[i].astype(BF16))
    out = _peer(xn2, tabs, peer_u[i].astype(BF16), peer_v[i].T.astype(BF16), h1, mod3,
                final_norm_g.reshape(1, d), seq)
    return out.reshape(batch, seq, d)
```

```python
import functools

import jax
import jax.numpy as jnp
from jax import lax
from jax.experimental import pallas as pl
from jax.experimental.pallas import tpu as pltpu

F32 = jnp.float32
BF16 = jnp.bfloat16

D_MODEL = 2048
GLA_HEADS = 4
GLA_DK = 128
GLA_DV = 256
GLA_KEY_WIDTH = GLA_HEADS * GLA_DK
GLA_WIDTH = GLA_HEADS * GLA_DV
GLA_GATE_RANK = 16
GLA_GATE_TAU = 16.0
GLA_CHUNK = 64
CMLP_GROUPS = 8
CMLP_GROUP_DIM = 128
CMLP_WIDTH = CMLP_GROUPS * CMLP_GROUP_DIM
CMLP_CHUNK = 128
PEER_HEADS = 8
PEER_NKEYS = 128
PEER_EXPERTS = PEER_NKEYS * PEER_NKEYS
PEER_QDIM = 256
PEER_HALF = PEER_QDIM // 2
PEER_TOPK = 16
N_MOD = 6
EPS = 1e-6

MAIN_WIDTH = 2 * GLA_KEY_WIDTH + 2 * GLA_WIDTH + 2 * CMLP_WIDTH
LR_PAD = 128
MOD_ROWS = 8

VMEM_LIMIT_BYTES = 56 * 1024 * 1024

_NT = (((1,), (1,)), ((), ()))
_TN = (((0,), (0,)), ((), ()))


def _params(*sem):
    return pltpu.CompilerParams(dimension_semantics=sem, vmem_limit_bytes=VMEM_LIMIT_BYTES)


def _gelu(x):
    return 0.5 * x * (1.0 + lax.erf(x * (2.0 ** -0.5)))


def _silu(x):
    return x * jax.nn.sigmoid(x)


def _mod_kernel(cc_ref, w_ref, b_ref, o_ref):
    s = _silu(cc_ref[...])
    o_ref[...] = jnp.dot(s, w_ref[...], precision=lax.Precision.HIGHEST,
                         preferred_element_type=F32) + b_ref[...]


def _modulation(cc, w_mod, b_mod):
    n = w_mod.shape[1]
    tn = 1024
    return pl.pallas_call(
        _mod_kernel,
        grid=(n // tn,),
        in_specs=[pl.BlockSpec((MOD_ROWS, D_MODEL), lambda j: (0, 0)),
                  pl.BlockSpec((D_MODEL, tn), lambda j: (0, j)),
                  pl.BlockSpec((1, tn), lambda j: (0, j))],
        out_specs=pl.BlockSpec((MOD_ROWS, tn), lambda j: (0, j)),
        out_shape=jax.ShapeDtypeStruct((MOD_ROWS, n), F32),
        compiler_params=_params("arbitrary"),
        name="mod",
    )(cc, w_mod, b_mod.reshape(1, n))


def _inproj_kernel(x_ref, mod_ref, g_ref, w_ref, wlr_ref, p_ref, lr_ref, xn_ref):
    @pl.when(pl.program_id(1) == 0)
    def _():
        x = x_ref[...]
        y = x * lax.rsqrt(jnp.mean(x * x, axis=-1, keepdims=True) + EPS) * g_ref[...]
        xn = (y * (1.0 + mod_ref[0, 1:2, :]) + mod_ref[0, 0:1, :]).astype(BF16)
        xn_ref[...] = xn
        lr_ref[...] = jnp.dot(xn, wlr_ref[...], preferred_element_type=F32).astype(lr_ref.dtype)

    p_ref[...] = jnp.dot(xn_ref[...], w_ref[...], preferred_element_type=F32).astype(p_ref.dtype)


def _inproj(x2, mod3, norm_g, w_main, w_lr, mod_row_of_tile):
    t = x2.shape[0]
    tm = min(512, t)
    tn = 1024
    return pl.pallas_call(
        _inproj_kernel,
        grid=(t // tm, MAIN_WIDTH // tn),
        in_specs=[pl.BlockSpec((tm, D_MODEL), lambda i, j: (i, 0)),
                  pl.BlockSpec((1, N_MOD, D_MODEL), lambda i, j: (mod_row_of_tile(i, tm), 0, 0)),
                  pl.BlockSpec((1, D_MODEL), lambda i, j: (0, 0)),
                  pl.BlockSpec((D_MODEL, tn), lambda i, j: (0, j)),
                  pl.BlockSpec((D_MODEL, LR_PAD), lambda i, j: (0, 0))],
        out_specs=[pl.BlockSpec((tm, tn), lambda i, j: (i, j)),
                   pl.BlockSpec((tm, LR_PAD), lambda i, j: (i, 0))],
        out_shape=[jax.ShapeDtypeStruct((t, MAIN_WIDTH), BF16),
                   jax.ShapeDtypeStruct((t, LR_PAD), BF16)],
        scratch_shapes=[pltpu.VMEM((tm, D_MODEL), BF16)],
        compiler_params=_params("arbitrary", "arbitrary"),
        name="inproj",
    )(x2, mod3, norm_g, w_main, w_lr)


def _gla_kernel(qf_ref, kf_ref, vf_ref, lrf_ref, qb_ref, kb_ref, vb_ref, lrb_ref,
                wup_ref, bg_ref, s0_ref, of_ref, ob_ref, st_ref, *, n_chunks):
    @pl.when(pl.program_id(1) == 0)
    def _():
        st_ref[...] = s0_ref[...]

    c = GLA_CHUNK
    row = lax.broadcasted_iota(jnp.int32, (c, c), 0)
    col = lax.broadcasted_iota(jnp.int32, (c, c), 1)
    keep = (col <= row, col >= row)
    q_scale = GLA_DK ** -0.5
    dirs = ((qf_ref, kf_ref, vf_ref, lrf_ref, of_ref), (qb_ref, kb_ref, vb_ref, lrb_ref, ob_ref))

    def chunk_step(ci, carry):
        for d, (q_ref, k_ref, v_ref, lr_ref, o_ref) in enumerate(dirs):
            cc = ci if d == 0 else n_chunks - 1 - ci
            r0 = pl.multiple_of(cc * c, c)
            rows = pl.ds(r0, c)
            z = jnp.dot(lr_ref[rows, :], wup_ref[d], preferred_element_type=F32) + bg_ref[d]
            la = (jnp.minimum(z, 0.0) - jnp.log1p(jnp.exp(-jnp.abs(z)))) * (1.0 / GLA_GATE_TAU)
            cum = jnp.dot(keep[d].astype(F32), la, precision=lax.Precision.HIGHEST,
                          preferred_element_type=F32)
            tot = cum[c - 1:c, :] if d == 0 else cum[0:1, :]
            qd = (q_ref[rows, :].astype(F32) * q_scale * jnp.exp(cum)).astype(BF16)
            kf = k_ref[rows, :].astype(F32)
            kd = (kf * jnp.exp(-cum)).astype(BF16)
            k2 = (kf * jnp.exp(tot - cum)).astype(BF16)
            etot = jnp.exp(tot)
            for h in range(GLA_HEADS):
                ks = slice(h * GLA_DK, (h + 1) * GLA_DK)
                vs = slice(h * GLA_DV, (h + 1) * GLA_DV)
                vh = v_ref[rows, vs]
                st = st_ref[0, d, h]
                sc = lax.dot_general(qd[:, ks], kd[:, ks], _NT, preferred_element_type=F32)
                sc = jnp.where(keep[d], sc, 0.0).astype(BF16)
                o = (jnp.dot(sc, vh, preferred_element_type=F32)
                     + lax.dot_general(qd[:, ks], st.astype(BF16), _NT, preferred_element_type=F32))
                o_ref[rows, vs] = o
                st_ref[0, d, h] = st * etot[:, ks] + lax.dot_general(
                    vh, k2[:, ks], _TN, preferred_element_type=F32)
        return carry

    lax.fori_loop(0, n_chunks, chunk_step, 0)


def _gla(p, lr, w_up, b_gate, s0, batch, seq):
    tb = min(512, seq)
    nb = seq // tb
    qw, vw = GLA_KEY_WIDTH, GLA_WIDTH
    fwd = lambda col: (lambda b, s: (b * nb + s, col))
    bwd = lambda col: (lambda b, s: (b * nb + nb - 1 - s, col))
    state_spec = pl.BlockSpec((1, 2, GLA_HEADS, GLA_DV, GLA_DK), lambda b, s: (b, 0, 0, 0, 0))
    in_specs = []
    for m in (fwd, bwd):
        in_specs += [pl.BlockSpec((tb, qw), m(0)), pl.BlockSpec((tb, qw), m(1)),
                     pl.BlockSpec((tb, vw), m(1)), pl.BlockSpec((tb, LR_PAD), m(0))]
    in_specs += [pl.BlockSpec((2, LR_PAD, qw), lambda b, s: (0, 0, 0)),
                 pl.BlockSpec((2, 1, qw), lambda b, s: (0, 0, 0)),
                 state_spec]
    return pl.pallas_call(
        functools.partial(_gla_kernel, n_chunks=tb // GLA_CHUNK),
        grid=(batch, nb),
        in_specs=in_specs,
        out_specs=[pl.BlockSpec((tb, vw), fwd(0)), pl.BlockSpec((tb, vw), bwd(0)), state_spec],
        out_shape=[jax.ShapeDtypeStruct((batch * seq, vw), F32),
                   jax.ShapeDtypeStruct((batch * seq, vw), F32),
                   jax.ShapeDtypeStruct((batch, 2, GLA_HEADS, GLA_DV, GLA_DK), F32)],
        compiler_params=_params("arbitrary", "arbitrary"),
        name="gla",
    )(p, p, p, lr, p, p, p, lr, w_up, b_gate, s0)


def _cmlp_kernel(u_ref, v_ref, lng_ref, lnb_ref, ws_ref, bs_ref, o_ref, *, n_chunks):
    gv = _gelu(v_ref[...].astype(F32))
    mu = jnp.mean(gv, axis=-1, keepdims=True)
    dv = gv - mu
    var = jnp.mean(dv * dv, axis=-1, keepdims=True)
    vn = (dv * lax.rsqrt(var + EPS) * lng_ref[...] + lnb_ref[...]).astype(BF16)
    for ch in range(n_chunks):
        rows = slice(ch * CMLP_CHUNK, (ch + 1) * CMLP_CHUNK)
        for g in range(CMLP_GROUPS):
            cols = slice(g * CMLP_GROUP_DIM, (g + 1) * CMLP_GROUP_DIM)
            s = jnp.dot(ws_ref[g], vn[rows, cols], preferred_element_type=F32) + bs_ref[:, g:g + 1]
            o_ref[rows, cols] = (_gelu(u_ref[rows, cols].astype(F32)) * s).astype(o_ref.dtype)


def _cmlp(p, ln_g, ln_b, w_s, b_st):
    t = p.shape[0]
    tm = min(512, t)
    cu_blk = (2 * GLA_KEY_WIDTH + 2 * GLA_WIDTH) // CMLP_WIDTH
    return pl.pallas_call(
        functools.partial(_cmlp_kernel, n_chunks=tm // CMLP_CHUNK),
        grid=(t // tm,),
        in_specs=[pl.BlockSpec((tm, CMLP_WIDTH), lambda i: (i, cu_blk)),
                  pl.BlockSpec((tm, CMLP_WIDTH), lambda i: (i, cu_blk + 1)),
                  pl.BlockSpec((1, CMLP_WIDTH), lambda i: (0, 0)),
                  pl.BlockSpec((1, CMLP_WIDTH), lambda i: (0, 0)),
                  pl.BlockSpec((CMLP_GROUPS, CMLP_CHUNK, CMLP_CHUNK), lambda i: (0, 0, 0)),
                  pl.BlockSpec((CMLP_CHUNK, CMLP_GROUPS), lambda i: (0, 0))],
        out_specs=pl.BlockSpec((tm, CMLP_WIDTH), lambda i: (i, 0)),
        out_shape=jax.ShapeDtypeStruct((t, CMLP_WIDTH), BF16),
        compiler_params=_params("arbitrary"),
        name="cmlp",
    )(p, p, ln_g, ln_b, w_s, b_st)


def _outproj_kernel(of_ref, ob_ref, g_ref, cm_ref, x_ref, mod_ref, gng_ref, n2g_ref, w_ref,
                    h1_ref, xn2_ref):
    o = of_ref[...] + ob_ref[...]
    heads = []
    for h in range(GLA_HEADS):
        oh = o[:, h * GLA_DV:(h + 1) * GLA_DV]
        heads.append(oh * lax.rsqrt(jnp.mean(oh * oh, axis=-1, keepdims=True) + EPS))
    on = jnp.concatenate(heads, axis=-1) * gng_ref[...]
    y = (on * _silu(g_ref[...].astype(F32))).astype(BF16)
    mix = (jnp.dot(y, w_ref[0:GLA_WIDTH, :], preferred_element_type=F32)
           + jnp.dot(cm_ref[...], w_ref[GLA_WIDTH:GLA_WIDTH + CMLP_WIDTH, :], preferred_element_type=F32))
    h1 = x_ref[...] + mod_ref[0, 2:3, :] * mix
    h1_ref[...] = h1
    hn = h1 * lax.rsqrt(jnp.mean(h1 * h1, axis=-1, keepdims=True) + EPS) * n2g_ref[...]
    xn2_ref[...] = (hn * (1.0 + mod_ref[0, 4:5, :]) + mod_ref[0, 3:4, :]).astype(xn2_ref.dtype)


def _outproj(o_f, o_b, p, cm, x2, mod3, gla_norm_g, norm2_g, w_out, seq):
    t = x2.shape[0]
    tm = min(512, t)
    g_blk = (2 * GLA_KEY_WIDTH + GLA_WIDTH) // GLA_WIDTH
    return pl.pallas_call(
        _outproj_kernel,
        grid=(t // tm,),
        in_specs=[pl.BlockSpec((tm, GLA_WIDTH), lambda i: (i, 0)),
                  pl.BlockSpec((tm, GLA_WIDTH), lambda i: (i, 0)),
                  pl.BlockSpec((tm, GLA_WIDTH), lambda i: (i, g_blk)),
                  pl.BlockSpec((tm, CMLP_WIDTH), lambda i: (i, 0)),
                  pl.BlockSpec((tm, D_MODEL), lambda i: (i, 0)),
                  pl.BlockSpec((1, N_MOD, D_MODEL), lambda i: ((i * tm) // seq, 0, 0)),
                  pl.BlockSpec((1, GLA_WIDTH), lambda i: (0, 0)),
                  pl.BlockSpec((1, D_MODEL), lambda i: (0, 0)),
                  pl.BlockSpec((GLA_WIDTH + CMLP_WIDTH, D_MODEL), lambda i: (0, 0))],
        out_specs=[pl.BlockSpec((tm, D_MODEL), lambda i: (i, 0)),
                   pl.BlockSpec((tm, D_MODEL), lambda i: (i, 0))],
        out_shape=[jax.ShapeDtypeStruct((t, D_MODEL), F32),
                   jax.ShapeDtypeStruct((t, D_MODEL), BF16)],
        compiler_params=_params("arbitrary"),
        name="outproj",
    )(o_f, o_b, p, cm, x2, mod3, gla_norm_g, norm2_g, w_out)


def _top16(s, key_iota):
    n = s.shape[1]
    neg = jnp.float32(-jnp.inf)
    slot = lax.broadcasted_iota(jnp.int32, (PEER_TOPK, n), 0)
    vals = jnp.zeros((PEER_TOPK, n), F32)
    rank = jnp.full(s.shape, float(PEER_TOPK), F32)
    for r in range(PEER_TOPK):
        m = jnp.max(s, axis=0, keepdims=True)
        idx = jnp.min(jnp.where(s == m, key_iota, float(s.shape[0])), axis=0, keepdims=True)
        sel = key_iota == idx
        rank = jnp.where(sel, float(r), rank)
        s = jnp.where(sel, neg, s)
        vals = jnp.where(slot == r, m, vals)
    return vals, rank


_CAND_INVALID = 1 << 20
_CAND_ROWS = 72


def _staircase_positions():
    assert PEER_TOPK == 16
    k, inv = PEER_TOPK, _CAND_INVALID
    pos = list(range(k))
    pos += [inv] + [a * k for a in range(1, k)]
    pos += [inv] + [k + b for b in range(1, 8)]
    pos += [inv, inv] + [a * k + 1 for a in range(2, 8)]
    pos += [2 * k + b if b in (2, 3, 4) else inv for b in range(8)]
    pos += [3 * k + b if b in (2, 3) else inv for b in range(8)]
    pos += [4 * k + b if b == 2 else inv for b in range(8)]
    assert len(pos) == _CAND_ROWS and sum(p != inv for p in pos) == 50
    return jnp.broadcast_to(jnp.array(pos, F32)[:, None], (_CAND_ROWS, ROUTE_LANES))


ROUTE_LANES = 128
ROUTE_TILES_PER_ITER = 2


def _route_kernel(xn_ref, wqt_ref, sk_ref, pos_ref, e1_ref, c1_ref, e2_ref, r2_ref, qt_ref):
    qt_ref[...] = lax.dot_general(wqt_ref[...], xn_ref[...], _NT, preferred_element_type=F32)
    n = ROUTE_LANES
    n_tiles = xn_ref.shape[0] // n
    k = PEER_TOPK
    key_iota = lax.broadcasted_iota(jnp.int32, (PEER_NKEYS, n), 0).astype(F32)
    row16 = lax.broadcasted_iota(jnp.int32, (k, n), 0)
    neg = jnp.float32(-jnp.inf)
    invalid = float(_CAND_INVALID)

    def tiles(t, carry):
        for u in range(ROUTE_TILES_PER_ITER):
            tile(t * ROUTE_TILES_PER_ITER + u)
        return carry

    def tile(t):
        h = t // n_tiles
        lanes = pl.ds(pl.multiple_of((t % n_tiles) * n, n), n)
        base = pl.multiple_of(h * PEER_QDIM, PEER_QDIM)
        q1 = qt_ref[pl.ds(base, PEER_HALF), lanes].astype(BF16)
        q2 = qt_ref[pl.ds(base + PEER_HALF, PEER_HALF), lanes].astype(BF16)
        s1 = jnp.dot(sk_ref[0, h], q1, preferred_element_type=F32)
        s2 = jnp.dot(sk_ref[1, h], q2, preferred_element_type=F32)
        v1, rank1 = _top16(s1, key_iota)
        v2, rank2 = _top16(s2, key_iota)
        pos = pos_ref[...]
        cand = jnp.concatenate([v1[0:1] + v2, v1 + v2[0:1], v1[1:2] + v2[0:8], v1[0:8] + v2[1:2],
                                v1[2:3] + v2[0:8], v1[3:4] + v2[0:8], v1[4:5] + v2[0:8]], axis=0)
        cand = jnp.where(pos < invalid, cand, neg)
        best = cand[0:1, :]
        rem = cand
        for _ in range(k):
            m = jnp.max(rem, axis=0, keepdims=True)
            idx = jnp.min(jnp.where(rem == m, pos, invalid), axis=0, keepdims=True)
            rem = jnp.where(pos == idx, neg, rem)
        chosen = rem != cand
        z = jnp.sum(jnp.where(chosen, jnp.exp(cand - best), 0.0), axis=0, keepdims=True)
        ch = chosen.astype(F32)
        rowsum = lambda lo, hi: jnp.sum(ch[lo:hi], axis=0, keepdims=True)
        cnt = ch[16:32] + jnp.concatenate([ch[40:48], jnp.zeros((8, n), F32)], axis=0)
        for a, (lo, hi) in enumerate(((0, 16), (32, 40), (48, 56), (56, 64), (64, 72))):
            cnt = cnt + jnp.where(row16 == a, rowsum(lo, hi), 0.0)
        c1 = jnp.zeros((PEER_NKEYS, n), F32)
        for a in range(k):
            c1 = jnp.where(rank1 == float(a), cnt[a:a + 1], c1)
        e1_ref[h, :, lanes] = jnp.exp(s1 - v1[0:1, :]) * (1.0 / z)
        c1_ref[h, :, lanes] = c1
        e2_ref[h, :, lanes] = jnp.exp(s2 - v2[0:1, :]).astype(e2_ref.dtype)
        r2_ref[h, :, lanes] = rank2.astype(r2_ref.dtype)

    lax.fori_loop(0, PEER_HEADS * n_tiles // ROUTE_TILES_PER_ITER, tiles, 0)


def _route(xn2, wq_t, sk):
    t = xn2.shape[0]
    tr = min(512, t)
    tab = lambda dt: jax.ShapeDtypeStruct((PEER_HEADS, PEER_NKEYS, t), dt)
    tab_spec = pl.BlockSpec((PEER_HEADS, PEER_NKEYS, tr), lambda i: (0, 0, i))
    return pl.pallas_call(
        _route_kernel,
        grid=(t // tr,),
        in_specs=[pl.BlockSpec((tr, D_MODEL), lambda i: (i, 0)),
                  pl.BlockSpec((PEER_HEADS * PEER_QDIM, D_MODEL), lambda i: (0, 0)),
                  pl.BlockSpec((2, PEER_HEADS, PEER_NKEYS, PEER_HALF), lambda i: (0, 0, 0, 0)),
                  pl.BlockSpec((_CAND_ROWS, ROUTE_LANES), lambda i: (0, 0))],
        out_specs=[tab_spec] * 4,
        out_shape=[tab(F32), tab(F32), tab(BF16), tab(BF16)],
        scratch_shapes=[pltpu.VMEM((PEER_HEADS * PEER_QDIM, tr), F32)],
        compiler_params=_params("arbitrary"),
        name="route",
    )(xn2, wq_t, sk, _staircase_positions())


PEER_CHUNK = 512
PEER_CHUNKS_PER_STEP = 2
PEER_TOKENS = 512
PEER_WEIGHT_LANES = 256
BF16_ROWS = 16


def _peer_kernel(xn_ref, e1_ref, c1_ref, e2_ref, r2_ref, u_ref, vt_ref, h1_ref, mod_ref, fg_ref,
                 o_ref, acc_ref, xs_ref):
    j = pl.program_id(1)
    ec = PEER_CHUNK
    n = xn_ref.shape[0]
    lanes = PEER_WEIGHT_LANES
    rows_per_chunk = ec // PEER_NKEYS
    rows_per_step = PEER_CHUNKS_PER_STEP * rows_per_chunk

    @pl.when(j == 0)
    def _():
        acc_ref[...] = jnp.zeros_like(acc_ref)
        xs_ref[...] = xn_ref[...]

    zero = jnp.zeros((), BF16)
    keys = pl.ds(pl.multiple_of(j * rows_per_step, rows_per_step), rows_per_step)
    chains = [(kk, tc) for kk in range(PEER_CHUNKS_PER_STEP) for tc in range(n // lanes)]

    def routing_weights(kk, tc):
        cols = slice(tc * lanes, (tc + 1) * lanes)
        row = (BF16_ROWS, lanes)
        reps = PEER_NKEYS // BF16_ROWS
        blocks = []
        for r in range(rows_per_chunk):
            i1 = kk * rows_per_chunk + r
            w = None
            for h in range(PEER_HEADS):
                e1 = jnp.broadcast_to(e1_ref[h, keys, cols][i1:i1 + 1], row).astype(BF16)
                c1 = jnp.broadcast_to(c1_ref[h, keys, cols][i1:i1 + 1], row).astype(BF16)
                e1 = jnp.concatenate([e1] * reps, axis=0)
                c1 = jnp.concatenate([c1] * reps, axis=0)
                t = e1 * jnp.where(r2_ref[h, :, cols] < c1, e2_ref[h, :, cols], zero)
                w = t if w is None else w + t
            blocks.append(w)
        return jnp.concatenate(blocks, axis=0)

    weights = [routing_weights(kk, tc) for kk, tc in chains]
    acts = [lax.dot_general(u_ref[kk * ec:(kk + 1) * ec, :], xs_ref[tc * lanes:(tc + 1) * lanes, :], _NT,
                            preferred_element_type=F32) for kk, tc in chains]
    for (kk, tc), w, act in zip(chains, weights, acts):
        p = w * _gelu(act).astype(BF16)
        acc_ref[:, tc * lanes:(tc + 1) * lanes] += jnp.dot(vt_ref[:, kk * ec:(kk + 1) * ec], p,
                                                           preferred_element_type=F32)

    @pl.when(j == pl.num_programs(1) - 1)
    def _():
        h2 = h1_ref[...] + mod_ref[0, 5:6, :] * acc_ref[...].T
        o_ref[...] = h2 * lax.rsqrt(jnp.mean(h2 * h2, axis=-1, keepdims=True) + EPS) * fg_ref[...]


def _peer(xn2, tabs, u, v_t, h1, mod3, final_g, seq):
    t = xn2.shape[0]
    tb = min(PEER_TOKENS, t)
    ecs = PEER_CHUNKS_PER_STEP * PEER_CHUNK
    tab_spec = pl.BlockSpec((PEER_HEADS, PEER_NKEYS, tb), lambda i, j: (0, 0, i))
    once = pl.Buffered(1)
    return pl.pallas_call(
        _peer_kernel,
        grid=(t // tb, PEER_EXPERTS // ecs),
        in_specs=[pl.BlockSpec((tb, D_MODEL), lambda i, j: (i, 0)),
                  tab_spec, tab_spec, tab_spec, tab_spec,
                  pl.BlockSpec((ecs, D_MODEL), lambda i, j: (j, 0)),
                  pl.BlockSpec((D_MODEL, ecs), lambda i, j: (0, j)),
                  pl.BlockSpec((tb, D_MODEL), lambda i, j: (i, 0), pipeline_mode=once),
                  pl.BlockSpec((1, N_MOD, D_MODEL), lambda i, j: ((i * tb) // seq, 0, 0)),
                  pl.BlockSpec((1, D_MODEL), lambda i, j: (0, 0))],
        out_specs=pl.BlockSpec((tb, D_MODEL), lambda i, j: (i, 0), pipeline_mode=once),
        out_shape=jax.ShapeDtypeStruct((t, D_MODEL), F32),
        scratch_shapes=[pltpu.VMEM((D_MODEL, tb), F32), pltpu.VMEM((tb, D_MODEL), BF16)],
        compiler_params=_params("arbitrary", "arbitrary"),
        name="peer",
    )(xn2, *tabs, u, v_t, h1, mod3, final_g)


def kernel(x, c, ctx, c_ctx, norm1_g, norm2_g, w_mod, b_mod, w_in, w_gate_up, b_gate, gla_norm_g,
           cmlp_ln_g, cmlp_ln_b, w_spatial, b_spatial, w_out, peer_wq, peer_sub_keys, peer_u, peer_v,
           final_norm_g):
    batch, seq, d = x.shape
    ctx_len = ctx.shape[1]
    depth = w_mod.shape[0]
    assert d == D_MODEL and depth == 1 and batch + 1 <= MOD_ROWS
    i = 0

    cc = jnp.zeros((MOD_ROWS, d), F32).at[:batch].set(c).at[batch].set(c_ctx)
    mod3 = _modulation(cc, w_mod[i], b_mod[i]).reshape(MOD_ROWS, N_MOD, d)

    qk_end = 2 * GLA_KEY_WIDTH + 2 * GLA_WIDTH
    lr_end = qk_end + 2 * GLA_GATE_RANK
    w = w_in[i]
    w_main = jnp.concatenate([w[:, :qk_end], w[:, lr_end:]], axis=1).astype(BF16)
    w_lr = jnp.pad(w[:, qk_end:lr_end], ((0, 0), (0, LR_PAD - 2 * GLA_GATE_RANK))).astype(BF16)
    n1 = norm1_g[i].reshape(1, d)
    x2 = x.reshape(batch * seq, d)
    p_lat, lr_lat = _inproj(x2, mod3, n1, w_main, w_lr, lambda t, tm: (t * tm) // seq)
    p_ctx, lr_ctx = _inproj(ctx.reshape(batch * ctx_len, d), mod3, n1, w_main, w_lr, lambda t, tm: batch)

    w_up = jnp.zeros((2, LR_PAD, GLA_KEY_WIDTH), F32)
    for dd in range(2):
        w_up = w_up.at[dd, dd * GLA_GATE_RANK:(dd + 1) * GLA_GATE_RANK].set(w_gate_up[i, dd])
    w_up = w_up.astype(BF16)
    bg = b_gate[i].reshape(2, 1, GLA_KEY_WIDTH)
    s_zero = jnp.zeros((batch, 2, GLA_HEADS, GLA_DV, GLA_DK), F32)
    _, _, s_ctx = _gla(p_ctx, lr_ctx, w_up, bg, s_zero, batch, ctx_len)
    o_f, o_b, _ = _gla(p_lat, lr_lat, w_up, bg, s_ctx, batch, seq)

    cm = _cmlp(p_lat, cmlp_ln_g[i].reshape(1, -1), cmlp_ln_b[i].reshape(1, -1),
               w_spatial[i].astype(BF16), b_spatial[i].T)

    h1, xn2 = _outproj(o_f, o_b, p_lat, cm, x2, mod3, gla_norm_g[i].reshape(1, -1),
                       norm2_g[i].reshape(1, d), w_out[i].astype(BF16), seq)

    tabs = _route(xn2, peer_wq[i].T.astype(BF16), peer_sub_keys[i].astype(BF16))
    out = _peer(xn2, tabs, peer_u[i].astype(BF16), peer_v[i].T.astype(BF16), h1, mod3,
                final_norm_g.reshape(1, d), seq)
    return out.reshape(batch, seq, d)
```

```python
import functools

import jax
import jax.numpy as jnp
from jax import lax
from jax.experimental import pallas as pl
from jax.experimental.pallas import tpu as pltpu

F32 = jnp.float32
BF16 = jnp.bfloat16

D_MODEL = 2048
GLA_HEADS = 4
GLA_DK = 128
GLA_DV = 256
GLA_KEY_WIDTH = GLA_HEADS * GLA_DK
GLA_WIDTH = GLA_HEADS * GLA_DV
GLA_GATE_RANK = 16
GLA_GATE_TAU = 16.0
GLA_CHUNK = 64
CMLP_GROUPS = 8
CMLP_GROUP_DIM = 128
CMLP_WIDTH = CMLP_GROUPS * CMLP_GROUP_DIM
CMLP_CHUNK = 128
PEER_HEADS = 8
PEER_NKEYS = 128
PEER_EXPERTS = PEER_NKEYS * PEER_NKEYS
PEER_QDIM = 256
PEER_HALF = PEER_QDIM // 2
PEER_TOPK = 16
N_MOD = 6
EPS = 1e-6

MAIN_WIDTH = 2 * GLA_KEY_WIDTH + 2 * GLA_WIDTH + 2 * CMLP_WIDTH
LR_PAD = 128
MOD_ROWS = 8

VMEM_LIMIT_BYTES = 56 * 1024 * 1024

_NT = (((1,), (1,)), ((), ()))
_TN = (((0,), (0,)), ((), ()))


def _params(*sem):
    return pltpu.CompilerParams(dimension_semantics=sem, vmem_limit_bytes=VMEM_LIMIT_BYTES)


def _gelu(x):
    return 0.5 * x * (1.0 + lax.erf(x * (2.0 ** -0.5)))


def _silu(x):
    return x * jax.nn.sigmoid(x)


def _mod_kernel(cc_ref, w_ref, b_ref, o_ref):
    s = _silu(cc_ref[...])
    o_ref[...] = jnp.dot(s, w_ref[...], precision=lax.Precision.HIGHEST,
                         preferred_element_type=F32) + b_ref[...]


def _modulation(cc, w_mod, b_mod):
    n = w_mod.shape[1]
    tn = 1024
    return pl.pallas_call(
        _mod_kernel,
        grid=(n // tn,),
        in_specs=[pl.BlockSpec((MOD_ROWS, D_MODEL), lambda j: (0, 0)),
                  pl.BlockSpec((D_MODEL, tn), lambda j: (0, j)),
                  pl.BlockSpec((1, tn), lambda j: (0, j))],
        out_specs=pl.BlockSpec((MOD_ROWS, tn), lambda j: (0, j)),
        out_shape=jax.ShapeDtypeStruct((MOD_ROWS, n), F32),
        compiler_params=_params("arbitrary"),
        name="mod",
    )(cc, w_mod, b_mod.reshape(1, n))


def _inproj_kernel(x_ref, mod_ref, g_ref, w_ref, wlr_ref, p_ref, lr_ref, xn_ref):
    @pl.when(pl.program_id(1) == 0)
    def _():
        x = x_ref[...]
        y = x * lax.rsqrt(jnp.mean(x * x, axis=-1, keepdims=True) + EPS) * g_ref[...]
        xn = (y * (1.0 + mod_ref[0, 1:2, :]) + mod_ref[0, 0:1, :]).astype(BF16)
        xn_ref[...] = xn
        lr_ref[...] = jnp.dot(xn, wlr_ref[...], preferred_element_type=F32).astype(lr_ref.dtype)

    p_ref[...] = jnp.dot(xn_ref[...], w_ref[...], preferred_element_type=F32).astype(p_ref.dtype)


def _inproj(x2, mod3, norm_g, w_main, w_lr, mod_row_of_tile):
    t = x2.shape[0]
    tm = min(1024, t)
    tn = 1024
    return pl.pallas_call(
        _inproj_kernel,
        grid=(t // tm, MAIN_WIDTH // tn),
        in_specs=[pl.BlockSpec((tm, D_MODEL), lambda i, j: (i, 0)),
                  pl.BlockSpec((1, N_MOD, D_MODEL), lambda i, j: (mod_row_of_tile(i, tm), 0, 0)),
                  pl.BlockSpec((1, D_MODEL), lambda i, j: (0, 0)),
                  pl.BlockSpec((D_MODEL, tn), lambda i, j: (0, j)),
                  pl.BlockSpec((D_MODEL, LR_PAD), lambda i, j: (0, 0))],
        out_specs=[pl.BlockSpec((tm, tn), lambda i, j: (i, j)),
                   pl.BlockSpec((tm, LR_PAD), lambda i, j: (i, 0))],
        out_shape=[jax.ShapeDtypeStruct((t, MAIN_WIDTH), BF16),
                   jax.ShapeDtypeStruct((t, LR_PAD), BF16)],
        scratch_shapes=[pltpu.VMEM((tm, D_MODEL), BF16)],
        compiler_params=_params("arbitrary", "arbitrary"),
        name="inproj",
    )(x2, mod3, norm_g, w_main, w_lr)


def _gla_kernel(qf_ref, kf_ref, vf_ref, lrf_ref, qb_ref, kb_ref, vb_ref, lrb_ref,
                wup_ref, bg_ref, s0_ref, of_ref, ob_ref, st_ref, *, n_chunks):
    @pl.when(pl.program_id(1) == 0)
    def _():
        st_ref[...] = s0_ref[...]

    c = GLA_CHUNK
    row = lax.broadcasted_iota(jnp.int32, (c, c), 0)
    col = lax.broadcasted_iota(jnp.int32, (c, c), 1)
    keep = (col <= row, col >= row)
    q_scale = GLA_DK ** -0.5
    dirs = ((qf_ref, kf_ref, vf_ref, lrf_ref, of_ref), (qb_ref, kb_ref, vb_ref, lrb_ref, ob_ref))

    def chunk_step(ci, carry):
        for d, (q_ref, k_ref, v_ref, lr_ref, o_ref) in enumerate(dirs):
            cc = ci if d == 0 else n_chunks - 1 - ci
            r0 = pl.multiple_of(cc * c, c)
            rows = pl.ds(r0, c)
            z = jnp.dot(lr_ref[rows, :], wup_ref[d], preferred_element_type=F32) + bg_ref[d]
            la = (jnp.minimum(z, 0.0) - jnp.log1p(jnp.exp(-jnp.abs(z)))) * (1.0 / GLA_GATE_TAU)
            cum = jnp.dot(keep[d].astype(F32), la, precision=lax.Precision.HIGHEST,
                          preferred_element_type=F32)
            tot = cum[c - 1:c, :] if d == 0 else cum[0:1, :]
            qd = (q_ref[rows, :].astype(F32) * q_scale * jnp.exp(cum)).astype(BF16)
            kf = k_ref[rows, :].astype(F32)
            kd = (kf * jnp.exp(-cum)).astype(BF16)
            k2 = (kf * jnp.exp(tot - cum)).astype(BF16)
            etot = jnp.exp(tot)
            for h in range(GLA_HEADS):
                ks = slice(h * GLA_DK, (h + 1) * GLA_DK)
                vs = slice(h * GLA_DV, (h + 1) * GLA_DV)
                vh = v_ref[rows, vs]
                st = st_ref[0, d, h]
                sc = lax.dot_general(qd[:, ks], kd[:, ks], _NT, preferred_element_type=F32)
                sc = jnp.where(keep[d], sc, 0.0).astype(BF16)
                o = (jnp.dot(sc, vh, preferred_element_type=F32)
                     + lax.dot_general(qd[:, ks], st.astype(BF16), _NT, preferred_element_type=F32))
                o_ref[rows, vs] = o
                st_ref[0, d, h] = st * etot[:, ks] + lax.dot_general(
                    vh, k2[:, ks], _TN, preferred_element_type=F32)
        return carry

    lax.fori_loop(0, n_chunks, chunk_step, 0)


def _gla(p, lr, w_up, b_gate, s0, batch, seq):
    tb = min(512, seq)
    nb = seq // tb
    qw, vw = GLA_KEY_WIDTH, GLA_WIDTH
    fwd = lambda col: (lambda b, s: (b * nb + s, col))
    bwd = lambda col: (lambda b, s: (b * nb + nb - 1 - s, col))
    state_spec = pl.BlockSpec((1, 2, GLA_HEADS, GLA_DV, GLA_DK), lambda b, s: (b, 0, 0, 0, 0))
    in_specs = []
    for m in (fwd, bwd):
        in_specs += [pl.BlockSpec((tb, qw), m(0)), pl.BlockSpec((tb, qw), m(1)),
                     pl.BlockSpec((tb, vw), m(1)), pl.BlockSpec((tb, LR_PAD), m(0))]
    in_specs += [pl.BlockSpec((2, LR_PAD, qw), lambda b, s: (0, 0, 0)),
                 pl.BlockSpec((2, 1, qw), lambda b, s: (0, 0, 0)),
                 state_spec]
    return pl.pallas_call(
        functools.partial(_gla_kernel, n_chunks=tb // GLA_CHUNK),
        grid=(batch, nb),
        in_specs=in_specs,
        out_specs=[pl.BlockSpec((tb, vw), fwd(0)), pl.BlockSpec((tb, vw), bwd(0)), state_spec],
        out_shape=[jax.ShapeDtypeStruct((batch * seq, vw), F32),
                   jax.ShapeDtypeStruct((batch * seq, vw), F32),
                   jax.ShapeDtypeStruct((batch, 2, GLA_HEADS, GLA_DV, GLA_DK), F32)],
        compiler_params=_params("arbitrary", "arbitrary"),
        name="gla",
    )(p, p, p, lr, p, p, p, lr, w_up, b_gate, s0)


def _cmlp_kernel(u_ref, v_ref, lng_ref, lnb_ref, ws_ref, bs_ref, o_ref, *, n_chunks):
    gv = _gelu(v_ref[...].astype(F32))
    mu = jnp.mean(gv, axis=-1, keepdims=True)
    dv = gv - mu
    var = jnp.mean(dv * dv, axis=-1, keepdims=True)
    vn = (dv * lax.rsqrt(var + EPS) * lng_ref[...] + lnb_ref[...]).astype(BF16)
    for ch in range(n_chunks):
        rows = slice(ch * CMLP_CHUNK, (ch + 1) * CMLP_CHUNK)
        for g in range(CMLP_GROUPS):
            cols = slice(g * CMLP_GROUP_DIM, (g + 1) * CMLP_GROUP_DIM)
            s = jnp.dot(ws_ref[g], vn[rows, cols], preferred_element_type=F32) + bs_ref[:, g:g + 1]
            o_ref[rows, cols] = (_gelu(u_ref[rows, cols].astype(F32)) * s).astype(o_ref.dtype)


def _cmlp(p, ln_g, ln_b, w_s, b_st):
    t = p.shape[0]
    tm = min(512, t)
    cu_blk = (2 * GLA_KEY_WIDTH + 2 * GLA_WIDTH) // CMLP_WIDTH
    return pl.pallas_call(
        functools.partial(_cmlp_kernel, n_chunks=tm // CMLP_CHUNK),
        grid=(t // tm,),
        in_specs=[pl.BlockSpec((tm, CMLP_WIDTH), lambda i: (i, cu_blk)),
                  pl.BlockSpec((tm, CMLP_WIDTH), lambda i: (i, cu_blk + 1)),
                  pl.BlockSpec((1, CMLP_WIDTH), lambda i: (0, 0)),
                  pl.BlockSpec((1, CMLP_WIDTH), lambda i: (0, 0)),
                  pl.BlockSpec((CMLP_GROUPS, CMLP_CHUNK, CMLP_CHUNK), lambda i: (0, 0, 0)),
                  pl.BlockSpec((CMLP_CHUNK, CMLP_GROUPS), lambda i: (0, 0))],
        out_specs=pl.BlockSpec((tm, CMLP_WIDTH), lambda i: (i, 0)),
        out_shape=jax.ShapeDtypeStruct((t, CMLP_WIDTH), BF16),
        compiler_params=_params("arbitrary"),
        name="cmlp",
    )(p, p, ln_g, ln_b, w_s, b_st)


def _outproj_kernel(of_ref, ob_ref, g_ref, cm_ref, x_ref, mod_ref, gng_ref, n2g_ref, w_ref,
                    h1_ref, xn2_ref):
    o = of_ref[...] + ob_ref[...]
    heads = []
    for h in range(GLA_HEADS):
        oh = o[:, h * GLA_DV:(h + 1) * GLA_DV]
        heads.append(oh * lax.rsqrt(jnp.mean(oh * oh, axis=-1, keepdims=True) + EPS))
    on = jnp.concatenate(heads, axis=-1) * gng_ref[...]
    y = (on * _silu(g_ref[...].astype(F32))).astype(BF16)
    mix = (jnp.dot(y, w_ref[0:GLA_WIDTH, :], preferred_element_type=F32)
           + jnp.dot(cm_ref[...], w_ref[GLA_WIDTH:GLA_WIDTH + CMLP_WIDTH, :], preferred_element_type=F32))
    h1 = x_ref[...] + mod_ref[0, 2:3, :] * mix
    h1_ref[...] = h1
    hn = h1 * lax.rsqrt(jnp.mean(h1 * h1, axis=-1, keepdims=True) + EPS) * n2g_ref[...]
    xn2_ref[...] = (hn * (1.0 + mod_ref[0, 4:5, :]) + mod_ref[0, 3:4, :]).astype(xn2_ref.dtype)


def _outproj(o_f, o_b, p, cm, x2, mod3, gla_norm_g, norm2_g, w_out, seq):
    t = x2.shape[0]
    tm = min(512, t)
    g_blk = (2 * GLA_KEY_WIDTH + GLA_WIDTH) // GLA_WIDTH
    return pl.pallas_call(
        _outproj_kernel,
        grid=(t // tm,),
        in_specs=[pl.BlockSpec((tm, GLA_WIDTH), lambda i: (i, 0)),
                  pl.BlockSpec((tm, GLA_WIDTH), lambda i: (i, 0)),
                  pl.BlockSpec((tm, GLA_WIDTH), lambda i: (i, g_blk)),
                  pl.BlockSpec((tm, CMLP_WIDTH), lambda i: (i, 0)),
                  pl.BlockSpec((tm, D_MODEL), lambda i: (i, 0)),
                  pl.BlockSpec((1, N_MOD, D_MODEL), lambda i: ((i * tm) // seq, 0, 0)),
                  pl.BlockSpec((1, GLA_WIDTH), lambda i: (0, 0)),
                  pl.BlockSpec((1, D_MODEL), lambda i: (0, 0)),
                  pl.BlockSpec((GLA_WIDTH + CMLP_WIDTH, D_MODEL), lambda i: (0, 0))],
        out_specs=[pl.BlockSpec((tm, D_MODEL), lambda i: (i, 0)),
                   pl.BlockSpec((tm, D_MODEL), lambda i: (i, 0))],
        out_shape=[jax.ShapeDtypeStruct((t, D_MODEL), F32),
                   jax.ShapeDtypeStruct((t, D_MODEL), BF16)],
        compiler_params=_params("arbitrary"),
        name="outproj",
    )(o_f, o_b, p, cm, x2, mod3, gla_norm_g, norm2_g, w_out)


def _top16(s, key_iota, exact):
    n = s.shape[1]
    neg = jnp.float32(-jnp.inf)
    slot = lax.broadcasted_iota(jnp.int32, (PEER_TOPK, n), 0)
    vals = jnp.zeros((PEER_TOPK, n), F32)
    rank = jnp.full(s.shape, float(PEER_TOPK), F32)
    for r in range(PEER_TOPK):
        m = jnp.max(s, axis=0, keepdims=True)
        sel = s == m
        if exact:
            idx = jnp.min(jnp.where(sel, key_iota, float(s.shape[0])), axis=0, keepdims=True)
            sel = key_iota == idx
        rank = jnp.where(sel, float(r), rank)
        s = jnp.where(sel, neg, s)
        vals = jnp.where(slot == r, m, vals)
    taken = jnp.sum(jnp.where(s == neg, 1.0, 0.0), axis=0, keepdims=True)
    return vals, rank, taken


_CAND_INVALID = 1 << 20
_CAND_ROWS = 72


def _staircase_positions():
    assert PEER_TOPK == 16
    k, inv = PEER_TOPK, _CAND_INVALID
    pos = list(range(k))
    pos += [inv] + [a * k for a in range(1, k)]
    pos += [inv] + [k + b for b in range(1, 8)]
    pos += [inv, inv] + [a * k + 1 for a in range(2, 8)]
    pos += [2 * k + b if b in (2, 3, 4) else inv for b in range(8)]
    pos += [3 * k + b if b in (2, 3) else inv for b in range(8)]
    pos += [4 * k + b if b == 2 else inv for b in range(8)]
    assert len(pos) == _CAND_ROWS and sum(p != inv for p in pos) == 50
    return jnp.broadcast_to(jnp.array(pos, F32)[:, None], (_CAND_ROWS, ROUTE_LANES))


ROUTE_LANES = 128
ROUTE_TILES_PER_ITER = 2


def _route_kernel(xn_ref, wqt_ref, sk_ref, pos_ref, e1_ref, c1_ref, e2_ref, r2_ref, qt_ref):
    qt_ref[...] = lax.dot_general(wqt_ref[...], xn_ref[...], _NT, preferred_element_type=F32)
    n = ROUTE_LANES
    n_tiles = xn_ref.shape[0] // n
    k = PEER_TOPK
    key_iota = lax.broadcasted_iota(jnp.int32, (PEER_NKEYS, n), 0).astype(F32)
    row16 = lax.broadcasted_iota(jnp.int32, (k, n), 0)
    neg = jnp.float32(-jnp.inf)
    invalid = float(_CAND_INVALID)

    def tiles(t, carry):
        first = t * ROUTE_TILES_PER_ITER
        taken = [tile(first + u, exact=False) for u in range(ROUTE_TILES_PER_ITER)]

        @pl.when(jnp.max(functools.reduce(jnp.maximum, taken)) > float(k))
        def _():
            for u in range(ROUTE_TILES_PER_ITER):
                tile(first + u, exact=True)
        return carry

    def tile(t, exact):
        h = t // n_tiles
        lanes = pl.ds(pl.multiple_of((t % n_tiles) * n, n), n)
        base = pl.multiple_of(h * PEER_QDIM, PEER_QDIM)
        q1 = qt_ref[pl.ds(base, PEER_HALF), lanes].astype(BF16)
        q2 = qt_ref[pl.ds(base + PEER_HALF, PEER_HALF), lanes].astype(BF16)
        s1 = jnp.dot(sk_ref[0, h], q1, preferred_element_type=F32)
        s2 = jnp.dot(sk_ref[1, h], q2, preferred_element_type=F32)
        v1, rank1, taken1 = _top16(s1, key_iota, exact)
        v2, rank2, taken2 = _top16(s2, key_iota, exact)
        pos = pos_ref[...]
        cand = jnp.concatenate([v1[0:1] + v2, v1 + v2[0:1], v1[1:2] + v2[0:8], v1[0:8] + v2[1:2],
                                v1[2:3] + v2[0:8], v1[3:4] + v2[0:8], v1[4:5] + v2[0:8]], axis=0)
        cand = jnp.where(pos < invalid, cand, neg)
        best = cand[0:1, :]
        rem = cand
        for _ in range(k):
            m = jnp.max(rem, axis=0, keepdims=True)
            sel = rem == m
            if exact:
                idx = jnp.min(jnp.where(sel, pos, invalid), axis=0, keepdims=True)
                sel = pos == idx
            rem = jnp.where(sel, neg, rem)
        chosen = rem != cand
        z = jnp.sum(jnp.where(chosen, jnp.exp(cand - best), 0.0), axis=0, keepdims=True)
        ch = chosen.astype(F32)
        rowsum = lambda lo, hi: jnp.sum(ch[lo:hi], axis=0, keepdims=True)
        cnt = ch[16:32] + jnp.concatenate([ch[40:48], jnp.zeros((8, n), F32)], axis=0)
        for a, (lo, hi) in enumerate(((0, 16), (32, 40), (48, 56), (56, 64), (64, 72))):
            cnt = cnt + jnp.where(row16 == a, rowsum(lo, hi), 0.0)
        c1 = jnp.zeros((PEER_NKEYS, n), F32)
        for a in range(k):
            c1 = jnp.where(rank1 == float(a), cnt[a:a + 1], c1)
        e1_ref[h, :, lanes] = jnp.exp(s1 - v1[0:1, :]) * (1.0 / z)
        c1_ref[h, :, lanes] = c1
        e2_ref[h, :, lanes] = jnp.exp(s2 - v2[0:1, :]).astype(e2_ref.dtype)
        r2_ref[h, :, lanes] = rank2.astype(r2_ref.dtype)
        taken_pairs = jnp.sum(ch, axis=0, keepdims=True)
        return jnp.maximum(jnp.maximum(taken1, taken2), taken_pairs)

    lax.fori_loop(0, PEER_HEADS * n_tiles // ROUTE_TILES_PER_ITER, tiles, 0)


def _route(xn2, wq_t, sk):
    t = xn2.shape[0]
    tr = min(512, t)
    tab = lambda dt: jax.ShapeDtypeStruct((PEER_HEADS, PEER_NKEYS, t), dt)
    tab_spec = pl.BlockSpec((PEER_HEADS, PEER_NKEYS, tr), lambda i: (0, 0, i))
    return pl.pallas_call(
        _route_kernel,
        grid=(t // tr,),
        in_specs=[pl.BlockSpec((tr, D_MODEL), lambda i: (i, 0)),
                  pl.BlockSpec((PEER_HEADS * PEER_QDIM, D_MODEL), lambda i: (0, 0)),
                  pl.BlockSpec((2, PEER_HEADS, PEER_NKEYS, PEER_HALF), lambda i: (0, 0, 0, 0)),
                  pl.BlockSpec((_CAND_ROWS, ROUTE_LANES), lambda i: (0, 0))],
        out_specs=[tab_spec] * 4,
        out_shape=[tab(F32), tab(F32), tab(BF16), tab(BF16)],
        scratch_shapes=[pltpu.VMEM((PEER_HEADS * PEER_QDIM, tr), F32)],
        compiler_params=_params("arbitrary"),
        name="route",
    )(xn2, wq_t, sk, _staircase_positions())


PEER_CHUNK = 512
PEER_CHUNKS_PER_STEP = 2
PEER_TOKENS = 512
PEER_WEIGHT_LANES = 256
BF16_ROWS = 16


def _peer_kernel(xn_ref, e1_ref, c1_ref, e2_ref, r2_ref, u_ref, vt_ref, h1_ref, mod_ref, fg_ref,
                 o_ref, acc_ref, xs_ref):
    j = pl.program_id(1)
    ec = PEER_CHUNK
    n = xn_ref.shape[0]
    lanes = PEER_WEIGHT_LANES
    rows_per_chunk = ec // PEER_NKEYS
    rows_per_step = PEER_CHUNKS_PER_STEP * rows_per_chunk

    @pl.when(j == 0)
    def _():
        acc_ref[...] = jnp.zeros_like(acc_ref)
        xs_ref[...] = xn_ref[...]

    zero = jnp.zeros((), BF16)
    keys = pl.ds(pl.multiple_of(j * rows_per_step, rows_per_step), rows_per_step)
    chains = [(kk, tc) for kk in range(PEER_CHUNKS_PER_STEP) for tc in range(n // lanes)]

    def routing_weights(kk, tc):
        cols = slice(tc * lanes, (tc + 1) * lanes)
        row = (BF16_ROWS, lanes)
        reps = PEER_NKEYS // BF16_ROWS
        blocks = []
        for r in range(rows_per_chunk):
            i1 = kk * rows_per_chunk + r
            w = None
            for h in range(PEER_HEADS):
                e1 = jnp.broadcast_to(e1_ref[h, keys, cols][i1:i1 + 1], row).astype(BF16)
                c1 = jnp.broadcast_to(c1_ref[h, keys, cols][i1:i1 + 1], row).astype(BF16)
                e1 = jnp.concatenate([e1] * reps, axis=0)
                c1 = jnp.concatenate([c1] * reps, axis=0)
                t = e1 * jnp.where(r2_ref[h, :, cols] < c1, e2_ref[h, :, cols], zero)
                w = t if w is None else w + t
            blocks.append(w)
        return jnp.concatenate(blocks, axis=0)

    acts = [lax.dot_general(u_ref[kk * ec:(kk + 1) * ec, :], xs_ref[tc * lanes:(tc + 1) * lanes, :], _NT,
                            preferred_element_type=F32) for kk, tc in chains]
    for (kk, tc), act in zip(chains, acts):
        p = routing_weights(kk, tc) * _gelu(act).astype(BF16)
        acc_ref[:, tc * lanes:(tc + 1) * lanes] += jnp.dot(vt_ref[:, kk * ec:(kk + 1) * ec], p,
                                                           preferred_element_type=F32)

    @pl.when(j == pl.num_programs(1) - 1)
    def _():
        h2 = h1_ref[...] + mod_ref[0, 5:6, :] * acc_ref[...].T
        o_ref[...] = h2 * lax.rsqrt(jnp.mean(h2 * h2, axis=-1, keepdims=True) + EPS) * fg_ref[...]


def _peer(xn2, tabs, u, v_t, h1, mod3, final_g, seq):
    t = xn2.shape[0]
    tb = min(PEER_TOKENS, t)
    ecs = PEER_CHUNKS_PER_STEP * PEER_CHUNK
    tab_spec = pl.BlockSpec((PEER_HEADS, PEER_NKEYS, tb), lambda i, j: (0, 0, i))
    once = pl.Buffered(1)
    return pl.pallas_call(
        _peer_kernel,
        grid=(t // tb, PEER_EXPERTS // ecs),
        in_specs=[pl.BlockSpec((tb, D_MODEL), lambda i, j: (i, 0)),
                  tab_spec, tab_spec, tab_spec, tab_spec,
                  pl.BlockSpec((ecs, D_MODEL), lambda i, j: (j, 0)),
                  pl.BlockSpec((D_MODEL, ecs), lambda i, j: (0, j)),
                  pl.BlockSpec((tb, D_MODEL), lambda i, j: (i, 0), pipeline_mode=once),
                  pl.BlockSpec((1, N_MOD, D_MODEL), lambda i, j: ((i * tb) // seq, 0, 0)),
                  pl.BlockSpec((1, D_MODEL), lambda i, j: (0, 0))],
        out_specs=pl.BlockSpec((tb, D_MODEL), lambda i, j: (i, 0), pipeline_mode=once),
        out_shape=jax.ShapeDtypeStruct((t, D_MODEL), F32),
        scratch_shapes=[pltpu.VMEM((D_MODEL, tb), F32), pltpu.VMEM((tb, D_MODEL), BF16)],
        compiler_params=_params("arbitrary", "arbitrary"),
        name="peer",
    )(xn2, *tabs, u, v_t, h1, mod3, final_g)


def kernel(x, c, ctx, c_ctx, norm1_g, norm2_g, w_mod, b_mod, w_in, w_gate_up, b_gate, gla_norm_g,
           cmlp_ln_g, cmlp_ln_b, w_spatial, b_spatial, w_out, peer_wq, peer_sub_keys, peer_u, peer_v,
           final_norm_g):
    batch, seq, d = x.shape
    ctx_len = ctx.shape[1]
    depth = w_mod.shape[0]
    assert d == D_MODEL and depth == 1 and batch + 1 <= MOD_ROWS
    i = 0

    cc = jnp.zeros((MOD_ROWS, d), F32).at[:batch].set(c).at[batch].set(c_ctx)
    mod3 = _modulation(cc, w_mod[i], b_mod[i]).reshape(MOD_ROWS, N_MOD, d)

    qk_end = 2 * GLA_KEY_WIDTH + 2 * GLA_WIDTH
    lr_end = qk_end + 2 * GLA_GATE_RANK
    w = w_in[i]
    w_main = jnp.concatenate([w[:, :qk_end], w[:, lr_end:]], axis=1).astype(BF16)
    w_lr = jnp.pad(w[:, qk_end:lr_end], ((0, 0), (0, LR_PAD - 2 * GLA_GATE_RANK))).astype(BF16)
    n1 = norm1_g[i].reshape(1, d)
    x2 = x.reshape(batch * seq, d)
    p_lat, lr_lat = _inproj(x2, mod3, n1, w_main, w_lr, lambda t, tm: (t * tm) // seq)
    p_ctx, lr_ctx = _inproj(ctx.reshape(batch * ctx_len, d), mod3, n1, w_main, w_lr, lambda t, tm: batch)

    w_up = jnp.zeros((2, LR_PAD, GLA_KEY_WIDTH), F32)
    for dd in range(2):
        w_up = w_up.at[dd, dd * GLA_GATE_RANK:(dd + 1) * GLA_GATE_RANK].set(w_gate_up[i, dd])
    w_up = w_up.astype(BF16)
    bg = b_gate[i].reshape(2, 1, GLA_KEY_WIDTH)
    s_zero = jnp.zeros((batch, 2, GLA_HEADS, GLA_DV, GLA_DK), F32)
    _, _, s_ctx = _gla(p_ctx, lr_ctx, w_up, bg, s_zero, batch, ctx_len)
    o_f, o_b, _ = _gla(p_lat, lr_lat, w_up, bg, s_ctx, batch, seq)

    cm = _cmlp(p_lat, cmlp_ln_g[i].reshape(1, -1), cmlp_ln_b[i].reshape(1, -1),
               w_spatial[i].astype(BF16), b_spatial[i].T)

    h1, xn2 = _outproj(o_f, o_b, p_lat, cm, x2, mod3, gla_norm_g[i].reshape(1, -1),
                       norm2_g[i].reshape(1, d), w_out[i].astype(BF16), seq)

    tabs = _route(xn2, peer_wq[i].T.astype(BF16), peer_sub_keys[i].astype(BF16))
    out = _peer(xn2, tabs, peer_u[i].astype(BF16), peer_v[i].T.astype(BF16), h1, mod3,
                final_norm_g.reshape(1, d), seq)
    return out.reshape(batch, seq, d)
```

```python
import functools

import jax
import jax.numpy as jnp
from jax import lax
from jax.experimental import pallas as pl
from jax.experimental.pallas import tpu as pltpu

F32 = jnp.float32
BF16 = jnp.bfloat16

D_MODEL = 2048
GLA_HEADS = 4
GLA_DK = 128
GLA_DV = 256
GLA_KEY_WIDTH = GLA_HEADS * GLA_DK
GLA_WIDTH = GLA_HEADS * GLA_DV
GLA_GATE_RANK = 16
GLA_GATE_TAU = 16.0
GLA_CHUNK = 64
CMLP_GROUPS = 8
CMLP_GROUP_DIM = 128
CMLP_WIDTH = CMLP_GROUPS * CMLP_GROUP_DIM
CMLP_CHUNK = 128
PEER_HEADS = 8
PEER_NKEYS = 128
PEER_EXPERTS = PEER_NKEYS * PEER_NKEYS
PEER_QDIM = 256
PEER_HALF = PEER_QDIM // 2
PEER_TOPK = 16
N_MOD = 6
EPS = 1e-6

MAIN_WIDTH = 2 * GLA_KEY_WIDTH + 2 * GLA_WIDTH + 2 * CMLP_WIDTH
LR_PAD = 128
MOD_ROWS = 8

VMEM_LIMIT_BYTES = 56 * 1024 * 1024

_NT = (((1,), (1,)), ((), ()))
_TN = (((0,), (0,)), ((), ()))


def _params(*sem):
    return pltpu.CompilerParams(dimension_semantics=sem, vmem_limit_bytes=VMEM_LIMIT_BYTES)


def _gelu(x):
    return 0.5 * x * (1.0 + lax.erf(x * (2.0 ** -0.5)))


def _silu(x):
    return x * jax.nn.sigmoid(x)


def _mod_kernel(cc_ref, w_ref, b_ref, o_ref):
    s = _silu(cc_ref[...])
    o_ref[...] = jnp.dot(s, w_ref[...], precision=lax.Precision.HIGHEST,
                         preferred_element_type=F32) + b_ref[...]


def _modulation(cc, w_mod, b_mod):
    n = w_mod.shape[1]
    tn = 1024
    return pl.pallas_call(
        _mod_kernel,
        grid=(n // tn,),
        in_specs=[pl.BlockSpec((MOD_ROWS, D_MODEL), lambda j: (0, 0)),
                  pl.BlockSpec((D_MODEL, tn), lambda j: (0, j)),
                  pl.BlockSpec((1, tn), lambda j: (0, j))],
        out_specs=pl.BlockSpec((MOD_ROWS, tn), lambda j: (0, j)),
        out_shape=jax.ShapeDtypeStruct((MOD_ROWS, n), F32),
        compiler_params=_params("arbitrary"),
        name="mod",
    )(cc, w_mod, b_mod.reshape(1, n))


def _inproj_kernel(x_ref, mod_ref, g_ref, w_ref, wlr_ref, p_ref, lr_ref, xn_ref):
    @pl.when(pl.program_id(1) == 0)
    def _():
        x = x_ref[...]
        y = x * lax.rsqrt(jnp.mean(x * x, axis=-1, keepdims=True) + EPS) * g_ref[...]
        xn = (y * (1.0 + mod_ref[0, 1:2, :]) + mod_ref[0, 0:1, :]).astype(BF16)
        xn_ref[...] = xn
        lr_ref[...] = jnp.dot(xn, wlr_ref[...], preferred_element_type=F32).astype(lr_ref.dtype)

    p_ref[...] = jnp.dot(xn_ref[...], w_ref[...], preferred_element_type=F32).astype(p_ref.dtype)


def _inproj(x2, mod3, norm_g, w_main, w_lr, mod_row_of_tile):
    t = x2.shape[0]
    tm = min(1024, t)
    tn = 1024
    return pl.pallas_call(
        _inproj_kernel,
        grid=(t // tm, MAIN_WIDTH // tn),
        in_specs=[pl.BlockSpec((tm, D_MODEL), lambda i, j: (i, 0)),
                  pl.BlockSpec((1, N_MOD, D_MODEL), lambda i, j: (mod_row_of_tile(i, tm), 0, 0)),
                  pl.BlockSpec((1, D_MODEL), lambda i, j: (0, 0)),
                  pl.BlockSpec((D_MODEL, tn), lambda i, j: (0, j)),
                  pl.BlockSpec((D_MODEL, LR_PAD), lambda i, j: (0, 0))],
        out_specs=[pl.BlockSpec((tm, tn), lambda i, j: (i, j)),
                   pl.BlockSpec((tm, LR_PAD), lambda i, j: (i, 0))],
        out_shape=[jax.ShapeDtypeStruct((t, MAIN_WIDTH), BF16),
                   jax.ShapeDtypeStruct((t, LR_PAD), BF16)],
        scratch_shapes=[pltpu.VMEM((tm, D_MODEL), BF16)],
        compiler_params=_params("arbitrary", "arbitrary"),
        name="inproj",
    )(x2, mod3, norm_g, w_main, w_lr)


def _gla_kernel(qf_ref, kf_ref, vf_ref, lrf_ref, qb_ref, kb_ref, vb_ref, lrb_ref,
                wup_ref, bg_ref, s0_ref, of_ref, ob_ref, st_ref, *, n_chunks):
    @pl.when(pl.program_id(1) == 0)
    def _():
        st_ref[...] = s0_ref[...]

    c = GLA_CHUNK
    row = lax.broadcasted_iota(jnp.int32, (c, c), 0)
    col = lax.broadcasted_iota(jnp.int32, (c, c), 1)
    keep = (col <= row, col >= row)
    q_scale = GLA_DK ** -0.5
    dirs = ((qf_ref, kf_ref, vf_ref, lrf_ref, of_ref), (qb_ref, kb_ref, vb_ref, lrb_ref, ob_ref))

    for d, (q_ref, k_ref, v_ref, lr_ref, o_ref) in enumerate(dirs):
        order = range(n_chunks) if d == 0 else range(n_chunks - 1, -1, -1)
        chunk = lambda ci: slice(ci * c, (ci + 1) * c)
        z = jnp.dot(lr_ref[...], wup_ref[d], preferred_element_type=F32) + bg_ref[d]
        la = (jnp.minimum(z, 0.0) - jnp.log1p(jnp.exp(-jnp.abs(z)))) * (1.0 / GLA_GATE_TAU)
        tri = keep[d].astype(F32)
        cums = [jnp.dot(tri, la[chunk(ci)], precision=lax.Precision.HIGHEST, preferred_element_type=F32)
                for ci in range(n_chunks)]
        tots = [cm[c - 1:c, :] if d == 0 else cm[0:1, :] for cm in cums]
        cum = jnp.concatenate(cums, axis=0)
        tot = jnp.concatenate([jnp.broadcast_to(t, (c, t.shape[1])) for t in tots], axis=0)
        kf = k_ref[...].astype(F32)
        qd = (q_ref[...].astype(F32) * q_scale * jnp.exp(cum)).astype(BF16)
        kd = (kf * jnp.exp(-cum)).astype(BF16)
        k2 = (kf * jnp.exp(tot - cum)).astype(BF16)
        etots = [jnp.exp(t) for t in tots]
        for h in range(GLA_HEADS):
            ks = slice(h * GLA_DK, (h + 1) * GLA_DK)
            vs = slice(h * GLA_DV, (h + 1) * GLA_DV)
            intra, incr = [], []
            for ci in range(n_chunks):
                rows = chunk(ci)
                vh = v_ref[rows, vs]
                sc = lax.dot_general(qd[rows, ks], kd[rows, ks], _NT, preferred_element_type=F32)
                sc = jnp.where(keep[d], sc, 0.0).astype(BF16)
                intra.append(jnp.dot(sc, vh, preferred_element_type=F32))
                incr.append(lax.dot_general(vh, k2[rows, ks], _TN, preferred_element_type=F32))
            st = st_ref[0, d, h]
            for ci in order:
                rows = chunk(ci)
                o_ref[rows, vs] = intra[ci] + lax.dot_general(qd[rows, ks], st.astype(BF16), _NT,
                                                              preferred_element_type=F32)
                st = st * etots[ci][:, ks] + incr[ci]
            st_ref[0, d, h] = st


def _gla(p, lr, w_up, b_gate, s0, batch, seq):
    tb = min(512, seq)
    nb = seq // tb
    qw, vw = GLA_KEY_WIDTH, GLA_WIDTH
    fwd = lambda col: (lambda b, s: (b * nb + s, col))
    bwd = lambda col: (lambda b, s: (b * nb + nb - 1 - s, col))
    state_spec = pl.BlockSpec((1, 2, GLA_HEADS, GLA_DV, GLA_DK), lambda b, s: (b, 0, 0, 0, 0))
    in_specs = []
    for m in (fwd, bwd):
        in_specs += [pl.BlockSpec((tb, qw), m(0)), pl.BlockSpec((tb, qw), m(1)),
                     pl.BlockSpec((tb, vw), m(1)), pl.BlockSpec((tb, LR_PAD), m(0))]
    in_specs += [pl.BlockSpec((2, LR_PAD, qw), lambda b, s: (0, 0, 0)),
                 pl.BlockSpec((2, 1, qw), lambda b, s: (0, 0, 0)),
                 state_spec]
    return pl.pallas_call(
        functools.partial(_gla_kernel, n_chunks=tb // GLA_CHUNK),
        grid=(batch, nb),
        in_specs=in_specs,
        out_specs=[pl.BlockSpec((tb, vw), fwd(0)), pl.BlockSpec((tb, vw), bwd(0)), state_spec],
        out_shape=[jax.ShapeDtypeStruct((batch * seq, vw), F32),
                   jax.ShapeDtypeStruct((batch * seq, vw), F32),
                   jax.ShapeDtypeStruct((batch, 2, GLA_HEADS, GLA_DV, GLA_DK), F32)],
        compiler_params=_params("arbitrary", "arbitrary"),
        name="gla",
    )(p, p, p, lr, p, p, p, lr, w_up, b_gate, s0)


def _cmlp_kernel(u_ref, v_ref, lng_ref, lnb_ref, ws_ref, bs_ref, o_ref, *, n_chunks):
    gv = _gelu(v_ref[...].astype(F32))
    mu = jnp.mean(gv, axis=-1, keepdims=True)
    dv = gv - mu
    var = jnp.mean(dv * dv, axis=-1, keepdims=True)
    vn = (dv * lax.rsqrt(var + EPS) * lng_ref[...] + lnb_ref[...]).astype(BF16)
    for ch in range(n_chunks):
        rows = slice(ch * CMLP_CHUNK, (ch + 1) * CMLP_CHUNK)
        for g in range(CMLP_GROUPS):
            cols = slice(g * CMLP_GROUP_DIM, (g + 1) * CMLP_GROUP_DIM)
            s = jnp.dot(ws_ref[g], vn[rows, cols], preferred_element_type=F32) + bs_ref[:, g:g + 1]
            o_ref[rows, cols] = (_gelu(u_ref[rows, cols].astype(F32)) * s).astype(o_ref.dtype)


def _cmlp(p, ln_g, ln_b, w_s, b_st):
    t = p.shape[0]
    tm = min(512, t)
    cu_blk = (2 * GLA_KEY_WIDTH + 2 * GLA_WIDTH) // CMLP_WIDTH
    return pl.pallas_call(
        functools.partial(_cmlp_kernel, n_chunks=tm // CMLP_CHUNK),
        grid=(t // tm,),
        in_specs=[pl.BlockSpec((tm, CMLP_WIDTH), lambda i: (i, cu_blk)),
                  pl.BlockSpec((tm, CMLP_WIDTH), lambda i: (i, cu_blk + 1)),
                  pl.BlockSpec((1, CMLP_WIDTH), lambda i: (0, 0)),
                  pl.BlockSpec((1, CMLP_WIDTH), lambda i: (0, 0)),
                  pl.BlockSpec((CMLP_GROUPS, CMLP_CHUNK, CMLP_CHUNK), lambda i: (0, 0, 0)),
                  pl.BlockSpec((CMLP_CHUNK, CMLP_GROUPS), lambda i: (0, 0))],
        out_specs=pl.BlockSpec((tm, CMLP_WIDTH), lambda i: (i, 0)),
        out_shape=jax.ShapeDtypeStruct((t, CMLP_WIDTH), BF16),
        compiler_params=_params("arbitrary"),
        name="cmlp",
    )(p, p, ln_g, ln_b, w_s, b_st)


def _outproj_kernel(of_ref, ob_ref, g_ref, cm_ref, x_ref, mod_ref, gng_ref, n2g_ref, w_ref,
                    h1_ref, xn2_ref):
    o = of_ref[...] + ob_ref[...]
    heads = []
    for h in range(GLA_HEADS):
        oh = o[:, h * GLA_DV:(h + 1) * GLA_DV]
        heads.append(oh * lax.rsqrt(jnp.mean(oh * oh, axis=-1, keepdims=True) + EPS))
    on = jnp.concatenate(heads, axis=-1) * gng_ref[...]
    y = (on * _silu(g_ref[...].astype(F32))).astype(BF16)
    mix = (jnp.dot(y, w_ref[0:GLA_WIDTH, :], preferred_element_type=F32)
           + jnp.dot(cm_ref[...], w_ref[GLA_WIDTH:GLA_WIDTH + CMLP_WIDTH, :], preferred_element_type=F32))
    h1 = x_ref[...] + mod_ref[0, 2:3, :] * mix
    h1_ref[...] = h1
    hn = h1 * lax.rsqrt(jnp.mean(h1 * h1, axis=-1, keepdims=True) + EPS) * n2g_ref[...]
    xn2_ref[...] = (hn * (1.0 + mod_ref[0, 4:5, :]) + mod_ref[0, 3:4, :]).astype(xn2_ref.dtype)


def _outproj(o_f, o_b, p, cm, x2, mod3, gla_norm_g, norm2_g, w_out, seq):
    t = x2.shape[0]
    tm = min(512, t)
    g_blk = (2 * GLA_KEY_WIDTH + GLA_WIDTH) // GLA_WIDTH
    return pl.pallas_call(
        _outproj_kernel,
        grid=(t // tm,),
        in_specs=[pl.BlockSpec((tm, GLA_WIDTH), lambda i: (i, 0)),
                  pl.BlockSpec((tm, GLA_WIDTH), lambda i: (i, 0)),
                  pl.BlockSpec((tm, GLA_WIDTH), lambda i: (i, g_blk)),
                  pl.BlockSpec((tm, CMLP_WIDTH), lambda i: (i, 0)),
                  pl.BlockSpec((tm, D_MODEL), lambda i: (i, 0)),
                  pl.BlockSpec((1, N_MOD, D_MODEL), lambda i: ((i * tm) // seq, 0, 0)),
                  pl.BlockSpec((1, GLA_WIDTH), lambda i: (0, 0)),
                  pl.BlockSpec((1, D_MODEL), lambda i: (0, 0)),
                  pl.BlockSpec((GLA_WIDTH + CMLP_WIDTH, D_MODEL), lambda i: (0, 0))],
        out_specs=[pl.BlockSpec((tm, D_MODEL), lambda i: (i, 0)),
                   pl.BlockSpec((tm, D_MODEL), lambda i: (i, 0))],
        out_shape=[jax.ShapeDtypeStruct((t, D_MODEL), F32),
                   jax.ShapeDtypeStruct((t, D_MODEL), BF16)],
        compiler_params=_params("arbitrary"),
        name="outproj",
    )(o_f, o_b, p, cm, x2, mod3, gla_norm_g, norm2_g, w_out)


def _top16(s, key_iota, exact):
    n = s.shape[1]
    neg = jnp.float32(-jnp.inf)
    slot = lax.broadcasted_iota(jnp.int32, (PEER_TOPK, n), 0)
    vals = jnp.zeros((PEER_TOPK, n), F32)
    rank = jnp.full(s.shape, float(PEER_TOPK), F32)
    for r in range(PEER_TOPK):
        m = jnp.max(s, axis=0, keepdims=True)
        sel = s == m
        if exact:
            idx = jnp.min(jnp.where(sel, key_iota, float(s.shape[0])), axis=0, keepdims=True)
            sel = key_iota == idx
        rank = jnp.where(sel, float(r), rank)
        s = jnp.where(sel, neg, s)
        vals = jnp.where(slot == r, m, vals)
    taken = jnp.sum(jnp.where(s == neg, 1.0, 0.0), axis=0, keepdims=True)
    return vals, rank, taken


_CAND_INVALID = 1 << 20
_CAND_ROWS = 72


def _staircase_positions():
    assert PEER_TOPK == 16
    k, inv = PEER_TOPK, _CAND_INVALID
    pos = list(range(k))
    pos += [inv] + [a * k for a in range(1, k)]
    pos += [inv] + [k + b for b in range(1, 8)]
    pos += [inv, inv] + [a * k + 1 for a in range(2, 8)]
    pos += [2 * k + b if b in (2, 3, 4) else inv for b in range(8)]
    pos += [3 * k + b if b in (2, 3) else inv for b in range(8)]
    pos += [4 * k + b if b == 2 else inv for b in range(8)]
    assert len(pos) == _CAND_ROWS and sum(p != inv for p in pos) == 50
    return jnp.broadcast_to(jnp.array(pos, F32)[:, None], (_CAND_ROWS, ROUTE_LANES))


ROUTE_LANES = 128
ROUTE_TILES_PER_ITER = 4


def _route_kernel(xn_ref, wqt_ref, sk_ref, pos_ref, e1_ref, c1_ref, e2_ref, r2_ref, qt_ref):
    qt_ref[...] = lax.dot_general(wqt_ref[...], xn_ref[...], _NT, preferred_element_type=F32)
    n = ROUTE_LANES
    n_tiles = xn_ref.shape[0] // n
    k = PEER_TOPK
    key_iota = lax.broadcasted_iota(jnp.int32, (PEER_NKEYS, n), 0).astype(F32)
    row16 = lax.broadcasted_iota(jnp.int32, (k, n), 0)
    neg = jnp.float32(-jnp.inf)
    invalid = float(_CAND_INVALID)

    def tiles(t, carry):
        first = t * ROUTE_TILES_PER_ITER
        taken = [tile(first + u, exact=False) for u in range(ROUTE_TILES_PER_ITER)]

        @pl.when(jnp.max(functools.reduce(jnp.maximum, taken)) > float(k))
        def _():
            for u in range(ROUTE_TILES_PER_ITER):
                tile(first + u, exact=True)
        return carry

    def tile(t, exact):
        h = t // n_tiles
        lanes = pl.ds(pl.multiple_of((t % n_tiles) * n, n), n)
        base = pl.multiple_of(h * PEER_QDIM, PEER_QDIM)
        q1 = qt_ref[pl.ds(base, PEER_HALF), lanes].astype(BF16)
        q2 = qt_ref[pl.ds(base + PEER_HALF, PEER_HALF), lanes].astype(BF16)
        s1 = jnp.dot(sk_ref[0, h], q1, preferred_element_type=F32)
        s2 = jnp.dot(sk_ref[1, h], q2, preferred_element_type=F32)
        v1, rank1, taken1 = _top16(s1, key_iota, exact)
        v2, rank2, taken2 = _top16(s2, key_iota, exact)
        pos = pos_ref[...]
        cand = jnp.concatenate([v1[0:1] + v2, v1 + v2[0:1], v1[1:2] + v2[0:8], v1[0:8] + v2[1:2],
                                v1[2:3] + v2[0:8], v1[3:4] + v2[0:8], v1[4:5] + v2[0:8]], axis=0)
        cand = jnp.where(pos < invalid, cand, neg)
        best = cand[0:1, :]
        rem = cand
        for _ in range(k):
            m = jnp.max(rem, axis=0, keepdims=True)
            sel = rem == m
            if exact:
                idx = jnp.min(jnp.where(sel, pos, invalid), axis=0, keepdims=True)
                sel = pos == idx
            rem = jnp.where(sel, neg, rem)
        chosen = rem != cand
        z = jnp.sum(jnp.where(chosen, jnp.exp(cand - best), 0.0), axis=0, keepdims=True)
        ch = chosen.astype(F32)
        rowsum = lambda lo, hi: jnp.sum(ch[lo:hi], axis=0, keepdims=True)
        cnt = ch[16:32] + jnp.concatenate([ch[40:48], jnp.zeros((8, n), F32)], axis=0)
        for a, (lo, hi) in enumerate(((0, 16), (32, 40), (48, 56), (56, 64), (64, 72))):
            cnt = cnt + jnp.where(row16 == a, rowsum(lo, hi), 0.0)
        c1 = jnp.zeros((PEER_NKEYS, n), F32)
        for a in range(k):
            c1 = jnp.where(rank1 == float(a), cnt[a:a + 1], c1)
        e1_ref[h, :, lanes] = jnp.exp(s1 - v1[0:1, :]) * (1.0 / z)
        c1_ref[h, :, lanes] = c1
        e2_ref[h, :, lanes] = jnp.exp(s2 - v2[0:1, :]).astype(e2_ref.dtype)
        r2_ref[h, :, lanes] = rank2.astype(r2_ref.dtype)
        taken_pairs = jnp.sum(ch, axis=0, keepdims=True)
        return jnp.maximum(jnp.maximum(taken1, taken2), taken_pairs)

    lax.fori_loop(0, PEER_HEADS * n_tiles // ROUTE_TILES_PER_ITER, tiles, 0)


def _route(xn2, wq_t, sk):
    t = xn2.shape[0]
    tr = min(512, t)
    tab = lambda dt: jax.ShapeDtypeStruct((PEER_HEADS, PEER_NKEYS, t), dt)
    tab_spec = pl.BlockSpec((PEER_HEADS, PEER_NKEYS, tr), lambda i: (0, 0, i))
    return pl.pallas_call(
        _route_kernel,
        grid=(t // tr,),
        in_specs=[pl.BlockSpec((tr, D_MODEL), lambda i: (i, 0)),
                  pl.BlockSpec((PEER_HEADS * PEER_QDIM, D_MODEL), lambda i: (0, 0)),
                  pl.BlockSpec((2, PEER_HEADS, PEER_NKEYS, PEER_HALF), lambda i: (0, 0, 0, 0)),
                  pl.BlockSpec((_CAND_ROWS, ROUTE_LANES), lambda i: (0, 0))],
        out_specs=[tab_spec] * 4,
        out_shape=[tab(F32), tab(F32), tab(BF16), tab(BF16)],
        scratch_shapes=[pltpu.VMEM((PEER_HEADS * PEER_QDIM, tr), F32)],
        compiler_params=_params("arbitrary"),
        name="route",
    )(xn2, wq_t, sk, _staircase_positions())


PEER_CHUNK = 512
PEER_CHUNKS_PER_STEP = 2
PEER_TOKENS = 512
PEER_WEIGHT_LANES = 256
BF16_ROWS = 16


def _peer_kernel(xn_ref, e1_ref, c1_ref, e2_ref, r2_ref, u_ref, vt_ref, h1_ref, mod_ref, fg_ref,
                 o_ref, acc_ref, xs_ref):
    j = pl.program_id(1)
    ec = PEER_CHUNK
    n = xn_ref.shape[0]
    lanes = PEER_WEIGHT_LANES
    rows_per_chunk = ec // PEER_NKEYS
    rows_per_step = PEER_CHUNKS_PER_STEP * rows_per_chunk

    @pl.when(j == 0)
    def _():
        acc_ref[...] = jnp.zeros_like(acc_ref)
        xs_ref[...] = xn_ref[...]

    zero = jnp.zeros((), BF16)
    keys = pl.ds(pl.multiple_of(j * rows_per_step, rows_per_step), rows_per_step)
    chains = [(kk, tc) for kk in range(PEER_CHUNKS_PER_STEP) for tc in range(n // lanes)]

    def routing_weights(kk, tc):
        cols = slice(tc * lanes, (tc + 1) * lanes)
        row = (BF16_ROWS, lanes)
        reps = PEER_NKEYS // BF16_ROWS
        blocks = []
        for r in range(rows_per_chunk):
            i1 = kk * rows_per_chunk + r
            w = None
            for h in range(PEER_HEADS):
                e1 = jnp.broadcast_to(e1_ref[h, keys, cols][i1:i1 + 1], row).astype(BF16)
                c1 = jnp.broadcast_to(c1_ref[h, keys, cols][i1:i1 + 1], row).astype(BF16)
                e1 = jnp.concatenate([e1] * reps, axis=0)
                c1 = jnp.concatenate([c1] * reps, axis=0)
                t = e1 * jnp.where(r2_ref[h, :, cols] < c1, e2_ref[h, :, cols], zero)
                w = t if w is None else w + t
            blocks.append(w)
        return jnp.concatenate(blocks, axis=0)

    acts = [lax.dot_general(u_ref[kk * ec:(kk + 1) * ec, :], xs_ref[tc * lanes:(tc + 1) * lanes, :], _NT,
                            preferred_element_type=F32) for kk, tc in chains]
    for (kk, tc), act in zip(chains, acts):
        p = routing_weights(kk, tc) * _gelu(act).astype(BF16)
        acc_ref[:, tc * lanes:(tc + 1) * lanes] += jnp.dot(vt_ref[:, kk * ec:(kk + 1) * ec], p,
                                                           preferred_element_type=F32)

    @pl.when(j == pl.num_programs(1) - 1)
    def _():
        h2 = h1_ref[...] + mod_ref[0, 5:6, :] * acc_ref[...].T
        o_ref[...] = h2 * lax.rsqrt(jnp.mean(h2 * h2, axis=-1, keepdims=True) + EPS) * fg_ref[...]


def _peer(xn2, tabs, u, v_t, h1, mod3, final_g, seq):
    t = xn2.shape[0]
    tb = min(PEER_TOKENS, t)
    ecs = PEER_CHUNKS_PER_STEP * PEER_CHUNK
    tab_spec = pl.BlockSpec((PEER_HEADS, PEER_NKEYS, tb), lambda i, j: (0, 0, i))
    once = pl.Buffered(1)
    return pl.pallas_call(
        _peer_kernel,
        grid=(t // tb, PEER_EXPERTS // ecs),
        in_specs=[pl.BlockSpec((tb, D_MODEL), lambda i, j: (i, 0)),
                  tab_spec, tab_spec, tab_spec, tab_spec,
                  pl.BlockSpec((ecs, D_MODEL), lambda i, j: (j, 0)),
                  pl.BlockSpec((D_MODEL, ecs), lambda i, j: (0, j)),
                  pl.BlockSpec((tb, D_MODEL), lambda i, j: (i, 0), pipeline_mode=once),
                  pl.BlockSpec((1, N_MOD, D_MODEL), lambda i, j: ((i * tb) // seq, 0, 0)),
                  pl.BlockSpec((1, D_MODEL), lambda i, j: (0, 0))],
        out_specs=pl.BlockSpec((tb, D_MODEL), lambda i, j: (i, 0), pipeline_mode=once),
        out_shape=jax.ShapeDtypeStruct((t, D_MODEL), F32),
        scratch_shapes=[pltpu.VMEM((D_MODEL, tb), F32), pltpu.VMEM((tb, D_MODEL), BF16)],
        compiler_params=_params("arbitrary", "arbitrary"),
        name="peer",
    )(xn2, *tabs, u, v_t, h1, mod3, final_g)


def kernel(x, c, ctx, c_ctx, norm1_g, norm2_g, w_mod, b_mod, w_in, w_gate_up, b_gate, gla_norm_g,
           cmlp_ln_g, cmlp_ln_b, w_spatial, b_spatial, w_out, peer_wq, peer_sub_keys, peer_u, peer_v,
           final_norm_g):
    batch, seq, d = x.shape
    ctx_len = ctx.shape[1]
    depth = w_mod.shape[0]
    assert d == D_MODEL and depth == 1 and batch + 1 <= MOD_ROWS
    i = 0

    cc = jnp.zeros((MOD_ROWS, d), F32).at[:batch].set(c).at[batch].set(c_ctx)
    mod3 = _modulation(cc, w_mod[i], b_mod[i]).reshape(MOD_ROWS, N_MOD, d)

    qk_end = 2 * GLA_KEY_WIDTH + 2 * GLA_WIDTH
    lr_end = qk_end + 2 * GLA_GATE_RANK
    w = w_in[i]
    w_main = jnp.concatenate([w[:, :qk_end], w[:, lr_end:]], axis=1).astype(BF16)
    w_lr = jnp.pad(w[:, qk_end:lr_end], ((0, 0), (0, LR_PAD - 2 * GLA_GATE_RANK))).astype(BF16)
    n1 = norm1_g[i].reshape(1, d)
    x2 = x.reshape(batch * seq, d)
    p_lat, lr_lat = _inproj(x2, mod3, n1, w_main, w_lr, lambda t, tm: (t * tm) // seq)
    p_ctx, lr_ctx = _inproj(ctx.reshape(batch * ctx_len, d), mod3, n1, w_main, w_lr, lambda t, tm: batch)

    w_up = jnp.zeros((2, LR_PAD, GLA_KEY_WIDTH), F32)
    for dd in range(2):
        w_up = w_up.at[dd, dd * GLA_GATE_RANK:(dd + 1) * GLA_GATE_RANK].set(w_gate_up[i, dd])
    w_up = w_up.astype(BF16)
    bg = b_gate[i].reshape(2, 1, GLA_KEY_WIDTH)
    s_zero = jnp.zeros((batch, 2, GLA_HEADS, GLA_DV, GLA_DK), F32)
    _, _, s_ctx = _gla(p_ctx, lr_ctx, w_up, bg, s_zero, batch, ctx_len)
    o_f, o_b, _ = _gla(p_lat, lr_lat, w_up, bg, s_ctx, batch, seq)

    cm = _cmlp(p_lat, cmlp_ln_g[i].reshape(1, -1), cmlp_ln_b[i].reshape(1, -1),
               w_spatial[i].astype(BF16), b_spatial[i].T)

    h1, xn2 = _outproj(o_f, o_b, p_lat, cm, x2, mod3, gla_norm_g[i].reshape(1, -1),
                       norm2_g[i].reshape(1, d), w_out[i].astype(BF16), seq)

    tabs = _route(xn2, peer_wq[i].T.astype(BF16), peer_sub_keys[i].astype(BF16))
    out = _peer(xn2, tabs, peer_u[i].astype(BF16), peer_v[i].T.astype(BF16), h1, mod3,
                final_norm_g.reshape(1, d), seq)
    return out.reshape(batch, seq, d)
```

```python
import functools

import jax
import jax.numpy as jnp
from jax import lax
from jax.experimental import pallas as pl
from jax.experimental.pallas import tpu as pltpu

F32 = jnp.float32
BF16 = jnp.bfloat16

D_MODEL = 2048
GLA_HEADS = 4
GLA_DK = 128
GLA_DV = 256
GLA_KEY_WIDTH = GLA_HEADS * GLA_DK
GLA_WIDTH = GLA_HEADS * GLA_DV
GLA_GATE_RANK = 16
GLA_GATE_TAU = 16.0
GLA_CHUNK = 64
CMLP_GROUPS = 8
CMLP_GROUP_DIM = 128
CMLP_WIDTH = CMLP_GROUPS * CMLP_GROUP_DIM
CMLP_CHUNK = 128
PEER_HEADS = 8
PEER_NKEYS = 128
PEER_EXPERTS = PEER_NKEYS * PEER_NKEYS
PEER_QDIM = 256
PEER_HALF = PEER_QDIM // 2
PEER_TOPK = 16
N_MOD = 6
EPS = 1e-6

MAIN_WIDTH = 2 * GLA_KEY_WIDTH + 2 * GLA_WIDTH + 2 * CMLP_WIDTH
LR_PAD = 128
MOD_ROWS = 8

VMEM_LIMIT_BYTES = 56 * 1024 * 1024

_NT = (((1,), (1,)), ((), ()))
_TN = (((0,), (0,)), ((), ()))


def _params(*sem):
    return pltpu.CompilerParams(dimension_semantics=sem, vmem_limit_bytes=VMEM_LIMIT_BYTES)


def _gelu(x):
    return 0.5 * x * (1.0 + lax.erf(x * (2.0 ** -0.5)))


def _silu(x):
    return x * jax.nn.sigmoid(x)


def _mod_kernel(cc_ref, w_ref, b_ref, o_ref):
    s = _silu(cc_ref[...])
    o_ref[...] = jnp.dot(s, w_ref[...], precision=lax.Precision.HIGHEST,
                         preferred_element_type=F32) + b_ref[...]


def _modulation(cc, w_mod, b_mod):
    n = w_mod.shape[1]
    tn = 1024
    return pl.pallas_call(
        _mod_kernel,
        grid=(n // tn,),
        in_specs=[pl.BlockSpec((MOD_ROWS, D_MODEL), lambda j: (0, 0)),
                  pl.BlockSpec((D_MODEL, tn), lambda j: (0, j)),
                  pl.BlockSpec((1, tn), lambda j: (0, j))],
        out_specs=pl.BlockSpec((MOD_ROWS, tn), lambda j: (0, j)),
        out_shape=jax.ShapeDtypeStruct((MOD_ROWS, n), F32),
        compiler_params=_params("arbitrary"),
        name="mod",
    )(cc, w_mod, b_mod.reshape(1, n))


def _inproj_kernel(x_ref, mod_ref, g_ref, w_ref, wlr_ref, p_ref, lr_ref, xn_ref):
    @pl.when(pl.program_id(1) == 0)
    def _():
        x = x_ref[...]
        y = x * lax.rsqrt(jnp.mean(x * x, axis=-1, keepdims=True) + EPS) * g_ref[...]
        xn = (y * (1.0 + mod_ref[0, 1:2, :]) + mod_ref[0, 0:1, :]).astype(BF16)
        xn_ref[...] = xn
        lr_ref[...] = jnp.dot(xn, wlr_ref[...], preferred_element_type=F32).astype(lr_ref.dtype)

    p_ref[...] = jnp.dot(xn_ref[...], w_ref[...], preferred_element_type=F32).astype(p_ref.dtype)


def _inproj(x2, mod3, norm_g, w_main, w_lr, mod_row_of_tile):
    t = x2.shape[0]
    tm = min(1024, t)
    tn = 1024
    return pl.pallas_call(
        _inproj_kernel,
        grid=(t // tm, MAIN_WIDTH // tn),
        in_specs=[pl.BlockSpec((tm, D_MODEL), lambda i, j: (i, 0)),
                  pl.BlockSpec((1, N_MOD, D_MODEL), lambda i, j: (mod_row_of_tile(i, tm), 0, 0)),
                  pl.BlockSpec((1, D_MODEL), lambda i, j: (0, 0)),
                  pl.BlockSpec((D_MODEL, tn), lambda i, j: (0, j)),
                  pl.BlockSpec((D_MODEL, LR_PAD), lambda i, j: (0, 0))],
        out_specs=[pl.BlockSpec((tm, tn), lambda i, j: (i, j)),
                   pl.BlockSpec((tm, LR_PAD), lambda i, j: (i, 0))],
        out_shape=[jax.ShapeDtypeStruct((t, MAIN_WIDTH), BF16),
                   jax.ShapeDtypeStruct((t, LR_PAD), BF16)],
        scratch_shapes=[pltpu.VMEM((tm, D_MODEL), BF16)],
        compiler_params=_params("arbitrary", "arbitrary"),
        name="inproj",
    )(x2, mod3, norm_g, w_main, w_lr)


def _gla_kernel(qf_ref, kf_ref, vf_ref, lrf_ref, qb_ref, kb_ref, vb_ref, lrb_ref,
                wup_ref, bg_ref, s0_ref, of_ref, ob_ref, st_ref, *, n_chunks):
    @pl.when(pl.program_id(1) == 0)
    def _():
        st_ref[...] = s0_ref[...]

    c = GLA_CHUNK
    row = lax.broadcasted_iota(jnp.int32, (c, c), 0)
    col = lax.broadcasted_iota(jnp.int32, (c, c), 1)
    keep = (col <= row, col >= row)
    q_scale = GLA_DK ** -0.5
    dirs = ((qf_ref, kf_ref, vf_ref, lrf_ref, of_ref), (qb_ref, kb_ref, vb_ref, lrb_ref, ob_ref))

    for d, (q_ref, k_ref, v_ref, lr_ref, o_ref) in enumerate(dirs):
        order = range(n_chunks) if d == 0 else range(n_chunks - 1, -1, -1)
        chunk = lambda ci: slice(ci * c, (ci + 1) * c)
        z = jnp.dot(lr_ref[...], wup_ref[d], preferred_element_type=F32) + bg_ref[d]
        la = (jnp.minimum(z, 0.0) - jnp.log1p(jnp.exp(-jnp.abs(z)))) * (1.0 / GLA_GATE_TAU)
        tri = keep[d].astype(F32)
        cums = [jnp.dot(tri, la[chunk(ci)], precision=lax.Precision.HIGHEST, preferred_element_type=F32)
                for ci in range(n_chunks)]
        tots = [cm[c - 1:c, :] if d == 0 else cm[0:1, :] for cm in cums]
        cum = jnp.concatenate(cums, axis=0)
        tot = jnp.concatenate([jnp.broadcast_to(t, (c, t.shape[1])) for t in tots], axis=0)
        kf = k_ref[...].astype(F32)
        qd = (q_ref[...].astype(F32) * q_scale * jnp.exp(cum)).astype(BF16)
        kd = (kf * jnp.exp(-cum)).astype(BF16)
        k2 = (kf * jnp.exp(tot - cum)).astype(BF16)
        etots = [jnp.exp(t) for t in tots]
        for h in range(GLA_HEADS):
            ks = slice(h * GLA_DK, (h + 1) * GLA_DK)
            vs = slice(h * GLA_DV, (h + 1) * GLA_DV)
            intra, incr = [], []
            for ci in range(n_chunks):
                rows = chunk(ci)
                vh = v_ref[rows, vs]
                sc = lax.dot_general(qd[rows, ks], kd[rows, ks], _NT, preferred_element_type=F32)
                sc = jnp.where(keep[d], sc, 0.0).astype(BF16)
                intra.append(jnp.dot(sc, vh, preferred_element_type=F32))
                incr.append(lax.dot_general(vh, k2[rows, ks], _TN, preferred_element_type=F32))
            st = st_ref[0, d, h]
            for ci in order:
                rows = chunk(ci)
                o_ref[rows, vs] = intra[ci] + lax.dot_general(qd[rows, ks], st.astype(BF16), _NT,
                                                              preferred_element_type=F32)
                st = st * etots[ci][:, ks] + incr[ci]
            st_ref[0, d, h] = st


def _gla(p, lr, w_up, b_gate, s0, batch, seq):
    tb = min(512, seq)
    nb = seq // tb
    qw, vw = GLA_KEY_WIDTH, GLA_WIDTH
    fwd = lambda col: (lambda b, s: (b * nb + s, col))
    bwd = lambda col: (lambda b, s: (b * nb + nb - 1 - s, col))
    state_spec = pl.BlockSpec((1, 2, GLA_HEADS, GLA_DV, GLA_DK), lambda b, s: (b, 0, 0, 0, 0))
    in_specs = []
    for m in (fwd, bwd):
        in_specs += [pl.BlockSpec((tb, qw), m(0)), pl.BlockSpec((tb, qw), m(1)),
                     pl.BlockSpec((tb, vw), m(1)), pl.BlockSpec((tb, LR_PAD), m(0))]
    in_specs += [pl.BlockSpec((2, LR_PAD, qw), lambda b, s: (0, 0, 0)),
                 pl.BlockSpec((2, 1, qw), lambda b, s: (0, 0, 0)),
                 state_spec]
    return pl.pallas_call(
        functools.partial(_gla_kernel, n_chunks=tb // GLA_CHUNK),
        grid=(batch, nb),
        in_specs=in_specs,
        out_specs=[pl.BlockSpec((tb, vw), fwd(0)), pl.BlockSpec((tb, vw), bwd(0)), state_spec],
        out_shape=[jax.ShapeDtypeStruct((batch * seq, vw), F32),
                   jax.ShapeDtypeStruct((batch * seq, vw), F32),
                   jax.ShapeDtypeStruct((batch, 2, GLA_HEADS, GLA_DV, GLA_DK), F32)],
        compiler_params=_params("arbitrary", "arbitrary"),
        name="gla",
    )(p, p, p, lr, p, p, p, lr, w_up, b_gate, s0)


def _cmlp_kernel(u_ref, v_ref, lng_ref, lnb_ref, ws_ref, bs_ref, o_ref, *, n_chunks):
    gv = _gelu(v_ref[...].astype(F32))
    mu = jnp.mean(gv, axis=-1, keepdims=True)
    dv = gv - mu
    var = jnp.mean(dv * dv, axis=-1, keepdims=True)
    vn = (dv * lax.rsqrt(var + EPS) * lng_ref[...] + lnb_ref[...]).astype(BF16)
    for ch in range(n_chunks):
        rows = slice(ch * CMLP_CHUNK, (ch + 1) * CMLP_CHUNK)
        for g in range(CMLP_GROUPS):
            cols = slice(g * CMLP_GROUP_DIM, (g + 1) * CMLP_GROUP_DIM)
            s = jnp.dot(ws_ref[g], vn[rows, cols], preferred_element_type=F32) + bs_ref[:, g:g + 1]
            o_ref[rows, cols] = (_gelu(u_ref[rows, cols].astype(F32)) * s).astype(o_ref.dtype)


def _cmlp(p, ln_g, ln_b, w_s, b_st):
    t = p.shape[0]
    tm = min(512, t)
    cu_blk = (2 * GLA_KEY_WIDTH + 2 * GLA_WIDTH) // CMLP_WIDTH
    return pl.pallas_call(
        functools.partial(_cmlp_kernel, n_chunks=tm // CMLP_CHUNK),
        grid=(t // tm,),
        in_specs=[pl.BlockSpec((tm, CMLP_WIDTH), lambda i: (i, cu_blk)),
                  pl.BlockSpec((tm, CMLP_WIDTH), lambda i: (i, cu_blk + 1)),
                  pl.BlockSpec((1, CMLP_WIDTH), lambda i: (0, 0)),
                  pl.BlockSpec((1, CMLP_WIDTH), lambda i: (0, 0)),
                  pl.BlockSpec((CMLP_GROUPS, CMLP_CHUNK, CMLP_CHUNK), lambda i: (0, 0, 0)),
                  pl.BlockSpec((CMLP_CHUNK, CMLP_GROUPS), lambda i: (0, 0))],
        out_specs=pl.BlockSpec((tm, CMLP_WIDTH), lambda i: (i, 0)),
        out_shape=jax.ShapeDtypeStruct((t, CMLP_WIDTH), BF16),
        compiler_params=_params("arbitrary"),
        name="cmlp",
    )(p, p, ln_g, ln_b, w_s, b_st)


def _outproj_kernel(of_ref, ob_ref, g_ref, cm_ref, x_ref, mod_ref, gng_ref, n2g_ref, w_ref,
                    h1_ref, xn2_ref):
    o = of_ref[...] + ob_ref[...]
    heads = []
    for h in range(GLA_HEADS):
        oh = o[:, h * GLA_DV:(h + 1) * GLA_DV]
        heads.append(oh * lax.rsqrt(jnp.mean(oh * oh, axis=-1, keepdims=True) + EPS))
    on = jnp.concatenate(heads, axis=-1) * gng_ref[...]
    y = (on * _silu(g_ref[...].astype(F32))).astype(BF16)
    mix = (jnp.dot(y, w_ref[0:GLA_WIDTH, :], preferred_element_type=F32)
           + jnp.dot(cm_ref[...], w_ref[GLA_WIDTH:GLA_WIDTH + CMLP_WIDTH, :], preferred_element_type=F32))
    h1 = x_ref[...] + mod_ref[0, 2:3, :] * mix
    h1_ref[...] = h1
    hn = h1 * lax.rsqrt(jnp.mean(h1 * h1, axis=-1, keepdims=True) + EPS) * n2g_ref[...]
    xn2_ref[...] = (hn * (1.0 + mod_ref[0, 4:5, :]) + mod_ref[0, 3:4, :]).astype(xn2_ref.dtype)


def _outproj(o_f, o_b, p, cm, x2, mod3, gla_norm_g, norm2_g, w_out, seq):
    t = x2.shape[0]
    tm = min(512, t)
    g_blk = (2 * GLA_KEY_WIDTH + GLA_WIDTH) // GLA_WIDTH
    return pl.pallas_call(
        _outproj_kernel,
        grid=(t // tm,),
        in_specs=[pl.BlockSpec((tm, GLA_WIDTH), lambda i: (i, 0)),
                  pl.BlockSpec((tm, GLA_WIDTH), lambda i: (i, 0)),
                  pl.BlockSpec((tm, GLA_WIDTH), lambda i: (i, g_blk)),
                  pl.BlockSpec((tm, CMLP_WIDTH), lambda i: (i, 0)),
                  pl.BlockSpec((tm, D_MODEL), lambda i: (i, 0)),
                  pl.BlockSpec((1, N_MOD, D_MODEL), lambda i: ((i * tm) // seq, 0, 0)),
                  pl.BlockSpec((1, GLA_WIDTH), lambda i: (0, 0)),
                  pl.BlockSpec((1, D_MODEL), lambda i: (0, 0)),
                  pl.BlockSpec((GLA_WIDTH + CMLP_WIDTH, D_MODEL), lambda i: (0, 0))],
        out_specs=[pl.BlockSpec((tm, D_MODEL), lambda i: (i, 0)),
                   pl.BlockSpec((tm, D_MODEL), lambda i: (i, 0))],
        out_shape=[jax.ShapeDtypeStruct((t, D_MODEL), F32),
                   jax.ShapeDtypeStruct((t, D_MODEL), BF16)],
        compiler_params=_params("arbitrary"),
        name="outproj",
    )(o_f, o_b, p, cm, x2, mod3, gla_norm_g, norm2_g, w_out)


def _top16(s, key_iota, exact):
    n = s.shape[1]
    neg = jnp.float32(-jnp.inf)
    slot = lax.broadcasted_iota(jnp.int32, (PEER_TOPK, n), 0)
    vals = jnp.zeros((PEER_TOPK, n), F32)
    rank = jnp.full(s.shape, float(PEER_TOPK), F32)
    for r in range(PEER_TOPK):
        m = jnp.max(s, axis=0, keepdims=True)
        sel = s == m
        if exact:
            idx = jnp.min(jnp.where(sel, key_iota, float(s.shape[0])), axis=0, keepdims=True)
            sel = key_iota == idx
        rank = jnp.where(sel, float(r), rank)
        s = jnp.where(sel, neg, s)
        vals = jnp.where(slot == r, m, vals)
    taken = jnp.sum(jnp.where(s == neg, 1.0, 0.0), axis=0, keepdims=True)
    return vals, rank, taken


_CAND_INVALID = 1 << 20
_CAND_ROWS = 72


def _staircase_positions():
    assert PEER_TOPK == 16
    k, inv = PEER_TOPK, _CAND_INVALID
    pos = list(range(k))
    pos += [inv] + [a * k for a in range(1, k)]
    pos += [inv] + [k + b for b in range(1, 8)]
    pos += [inv, inv] + [a * k + 1 for a in range(2, 8)]
    pos += [2 * k + b if b in (2, 3, 4) else inv for b in range(8)]
    pos += [3 * k + b if b in (2, 3) else inv for b in range(8)]
    pos += [4 * k + b if b == 2 else inv for b in range(8)]
    assert len(pos) == _CAND_ROWS and sum(p != inv for p in pos) == 50
    return jnp.broadcast_to(jnp.array(pos, F32)[:, None], (_CAND_ROWS, ROUTE_LANES))


ROUTE_LANES = 128
ROUTE_TILES_PER_ITER = 4


def _route_kernel(xn_ref, wqt_ref, sk_ref, pos_ref, e1_ref, c1_ref, e2_ref, r2_ref, qt_ref):
    qt_ref[...] = lax.dot_general(wqt_ref[...], xn_ref[...], _NT, preferred_element_type=F32)
    n = ROUTE_LANES
    n_tiles = xn_ref.shape[0] // n
    k = PEER_TOPK
    key_iota = lax.broadcasted_iota(jnp.int32, (PEER_NKEYS, n), 0).astype(F32)
    row16 = lax.broadcasted_iota(jnp.int32, (k, n), 0)
    neg = jnp.float32(-jnp.inf)
    invalid = float(_CAND_INVALID)

    def tiles(t, carry):
        first = t * ROUTE_TILES_PER_ITER
        taken = [tile(first + u, exact=False) for u in range(ROUTE_TILES_PER_ITER)]

        @pl.when(jnp.max(functools.reduce(jnp.maximum, taken)) > float(k))
        def _():
            for u in range(ROUTE_TILES_PER_ITER):
                tile(first + u, exact=True)
        return carry

    def tile(t, exact):
        h = t // n_tiles
        lanes = pl.ds(pl.multiple_of((t % n_tiles) * n, n), n)
        base = pl.multiple_of(h * PEER_QDIM, PEER_QDIM)
        q1 = qt_ref[pl.ds(base, PEER_HALF), lanes].astype(BF16)
        q2 = qt_ref[pl.ds(base + PEER_HALF, PEER_HALF), lanes].astype(BF16)
        s1 = jnp.dot(sk_ref[0, h], q1, preferred_element_type=F32)
        s2 = jnp.dot(sk_ref[1, h], q2, preferred_element_type=F32)
        v1, rank1, taken1 = _top16(s1, key_iota, exact)
        v2, rank2, taken2 = _top16(s2, key_iota, exact)
        pos = pos_ref[...]
        cand = jnp.concatenate([v1[0:1] + v2, v1 + v2[0:1], v1[1:2] + v2[0:8], v1[0:8] + v2[1:2],
                                v1[2:3] + v2[0:8], v1[3:4] + v2[0:8], v1[4:5] + v2[0:8]], axis=0)
        cand = jnp.where(pos < invalid, cand, neg)
        best = cand[0:1, :]
        rem = cand
        for _ in range(k):
            m = jnp.max(rem, axis=0, keepdims=True)
            sel = rem == m
            if exact:
                idx = jnp.min(jnp.where(sel, pos, invalid), axis=0, keepdims=True)
                sel = pos == idx
            rem = jnp.where(sel, neg, rem)
        chosen = rem != cand
        z = jnp.sum(jnp.where(chosen, jnp.exp(cand - best), 0.0), axis=0, keepdims=True)
        ch = chosen.astype(F32)
        rowsum = lambda lo, hi: jnp.sum(ch[lo:hi], axis=0, keepdims=True)
        cnt = ch[16:32] + jnp.concatenate([ch[40:48], jnp.zeros((8, n), F32)], axis=0)
        for a, (lo, hi) in enumerate(((0, 16), (32, 40), (48, 56), (56, 64), (64, 72))):
            cnt = cnt + jnp.where(row16 == a, rowsum(lo, hi), 0.0)
        c1 = jnp.zeros((PEER_NKEYS, n), F32)
        for a in range(k):
            c1 = jnp.where(rank1 == float(a), cnt[a:a + 1], c1)
        e1_ref[h, :, lanes] = jnp.exp(s1 - v1[0:1, :]) * (1.0 / z)
        c1_ref[h, :, lanes] = c1
        e2_ref[h, :, lanes] = jnp.exp(s2 - v2[0:1, :]).astype(e2_ref.dtype)
        r2_ref[h, :, lanes] = rank2.astype(r2_ref.dtype)
        taken_pairs = jnp.sum(ch, axis=0, keepdims=True)
        return jnp.maximum(jnp.maximum(taken1, taken2), taken_pairs)

    lax.fori_loop(0, PEER_HEADS * n_tiles // ROUTE_TILES_PER_ITER, tiles, 0)


def _route(xn2, wq_t, sk):
    t = xn2.shape[0]
    tr = min(512, t)
    tab = lambda dt: jax.ShapeDtypeStruct((PEER_HEADS, PEER_NKEYS, t), dt)
    tab_spec = pl.BlockSpec((PEER_HEADS, PEER_NKEYS, tr), lambda i: (0, 0, i))
    return pl.pallas_call(
        _route_kernel,
        grid=(t // tr,),
        in_specs=[pl.BlockSpec((tr, D_MODEL), lambda i: (i, 0)),
                  pl.BlockSpec((PEER_HEADS * PEER_QDIM, D_MODEL), lambda i: (0, 0)),
                  pl.BlockSpec((2, PEER_HEADS, PEER_NKEYS, PEER_HALF), lambda i: (0, 0, 0, 0)),
                  pl.BlockSpec((_CAND_ROWS, ROUTE_LANES), lambda i: (0, 0))],
        out_specs=[tab_spec] * 4,
        out_shape=[tab(F32), tab(F32), tab(BF16), tab(BF16)],
        scratch_shapes=[pltpu.VMEM((PEER_HEADS * PEER_QDIM, tr), F32)],
        compiler_params=_params("arbitrary"),
        name="route",
    )(xn2, wq_t, sk, _staircase_positions())


PEER_CHUNK = 512
PEER_CHUNKS_PER_STEP = 2
PEER_TOKENS = 512
PEER_WEIGHT_LANES = 256
BF16_ROWS = 16
PEER_VMEM_LIMIT_BYTES = 63 * 1024 * 1024


def _peer_kernel(xn_ref, e1_ref, c1_ref, e2_ref, r2_ref, u_ref, vt_ref, h1_ref, mod_ref, fg_ref,
                 o_ref, acc_ref, xs_ref):
    j = pl.program_id(1)
    ec = PEER_CHUNK
    n = xn_ref.shape[0]
    lanes = PEER_WEIGHT_LANES
    rows_per_chunk = ec // PEER_NKEYS
    rows_per_step = PEER_CHUNKS_PER_STEP * rows_per_chunk

    @pl.when(j == 0)
    def _():
        acc_ref[...] = jnp.zeros_like(acc_ref)
        xs_ref[...] = xn_ref[...]

    zero = jnp.zeros((), BF16)
    keys = pl.ds(pl.multiple_of(j * rows_per_step, rows_per_step), rows_per_step)
    chains = [(kk, tc) for kk in range(PEER_CHUNKS_PER_STEP) for tc in range(n // lanes)]

    def routing_weights(kk, tc):
        cols = slice(tc * lanes, (tc + 1) * lanes)
        row = (BF16_ROWS, lanes)
        reps = PEER_NKEYS // BF16_ROWS
        blocks = []
        for r in range(rows_per_chunk):
            i1 = kk * rows_per_chunk + r
            w = None
            for h in range(PEER_HEADS):
                e1 = jnp.broadcast_to(e1_ref[h, keys, cols][i1:i1 + 1], row).astype(BF16)
                c1 = jnp.broadcast_to(c1_ref[h, keys, cols][i1:i1 + 1], row).astype(BF16)
                e1 = jnp.concatenate([e1] * reps, axis=0)
                c1 = jnp.concatenate([c1] * reps, axis=0)
                t = e1 * jnp.where(r2_ref[h, :, cols] < c1, e2_ref[h, :, cols], zero)
                w = t if w is None else w + t
            blocks.append(w)
        return jnp.concatenate(blocks, axis=0)

    acts = [lax.dot_general(u_ref[kk * ec:(kk + 1) * ec, :], xs_ref[tc * lanes:(tc + 1) * lanes, :], _NT,
                            preferred_element_type=F32) for kk, tc in chains]
    for (kk, tc), act in zip(chains, acts):
        p = routing_weights(kk, tc) * _gelu(act).astype(BF16)
        acc_ref[:, tc * lanes:(tc + 1) * lanes] += jnp.dot(vt_ref[:, kk * ec:(kk + 1) * ec], p,
                                                           preferred_element_type=F32)

    @pl.when(j == pl.num_programs(1) - 1)
    def _():
        h2 = h1_ref[...] + mod_ref[0, 5:6, :] * acc_ref[...].T
        o_ref[...] = h2 * lax.rsqrt(jnp.mean(h2 * h2, axis=-1, keepdims=True) + EPS) * fg_ref[...]


def _peer(xn2, tabs, u, v_t, h1, mod3, final_g, seq):
    t = xn2.shape[0]
    tb = min(PEER_TOKENS, t)
    ecs = PEER_CHUNKS_PER_STEP * PEER_CHUNK
    tab_spec = pl.BlockSpec((PEER_HEADS, PEER_NKEYS, tb), lambda i, j: (0, 0, i))
    return pl.pallas_call(
        _peer_kernel,
        grid=(t // tb, PEER_EXPERTS // ecs),
        in_specs=[pl.BlockSpec((tb, D_MODEL), lambda i, j: (i, 0)),
                  tab_spec, tab_spec, tab_spec, tab_spec,
                  pl.BlockSpec((ecs, D_MODEL), lambda i, j: (j, 0)),
                  pl.BlockSpec((D_MODEL, ecs), lambda i, j: (0, j)),
                  pl.BlockSpec((tb, D_MODEL), lambda i, j: (i, 0)),
                  pl.BlockSpec((1, N_MOD, D_MODEL), lambda i, j: ((i * tb) // seq, 0, 0)),
                  pl.BlockSpec((1, D_MODEL), lambda i, j: (0, 0))],
        out_specs=pl.BlockSpec((tb, D_MODEL), lambda i, j: (i, 0)),
        out_shape=jax.ShapeDtypeStruct((t, D_MODEL), F32),
        scratch_shapes=[pltpu.VMEM((D_MODEL, tb), F32), pltpu.VMEM((tb, D_MODEL), BF16)],
        compiler_params=pltpu.CompilerParams(dimension_semantics=("arbitrary", "arbitrary"),
                                             vmem_limit_bytes=PEER_VMEM_LIMIT_BYTES),
        name="peer",
    )(xn2, *tabs, u, v_t, h1, mod3, final_g)


def kernel(x, c, ctx, c_ctx, norm1_g, norm2_g, w_mod, b_mod, w_in, w_gate_up, b_gate, gla_norm_g,
           cmlp_ln_g, cmlp_ln_b, w_spatial, b_spatial, w_out, peer_wq, peer_sub_keys, peer_u, peer_v,
           final_norm_g):
    batch, seq, d = x.shape
    ctx_len = ctx.shape[1]
    depth = w_mod.shape[0]
    assert d == D_MODEL and depth == 1 and batch + 1 <= MOD_ROWS
    i = 0

    cc = jnp.zeros((MOD_ROWS, d), F32).at[:batch].set(c).at[batch].set(c_ctx)
    mod3 = _modulation(cc, w_mod[i], b_mod[i]).reshape(MOD_ROWS, N_MOD, d)

    qk_end = 2 * GLA_KEY_WIDTH + 2 * GLA_WIDTH
    lr_end = qk_end + 2 * GLA_GATE_RANK
    w = w_in[i]
    w_main = jnp.concatenate([w[:, :qk_end], w[:, lr_end:]], axis=1).astype(BF16)
    w_lr = jnp.pad(w[:, qk_end:lr_end], ((0, 0), (0, LR_PAD - 2 * GLA_GATE_RANK))).astype(BF16)
    n1 = norm1_g[i].reshape(1, d)
    x2 = x.reshape(batch * seq, d)
    p_lat, lr_lat = _inproj(x2, mod3, n1, w_main, w_lr, lambda t, tm: (t * tm) // seq)
    p_ctx, lr_ctx = _inproj(ctx.reshape(batch * ctx_len, d), mod3, n1, w_main, w_lr, lambda t, tm: batch)

    w_up = jnp.zeros((2, LR_PAD, GLA_KEY_WIDTH), F32)
    for dd in range(2):
        w_up = w_up.at[dd, dd * GLA_GATE_RANK:(dd + 1) * GLA_GATE_RANK].set(w_gate_up[i, dd])
    w_up = w_up.astype(BF16)
    bg = b_gate[i].reshape(2, 1, GLA_KEY_WIDTH)
    s_zero = jnp.zeros((batch, 2, GLA_HEADS, GLA_DV, GLA_DK), F32)
    _, _, s_ctx = _gla(p_ctx, lr_ctx, w_up, bg, s_zero, batch, ctx_len)
    o_f, o_b, _ = _gla(p_lat, lr_lat, w_up, bg, s_ctx, batch, seq)

    cm = _cmlp(p_lat, cmlp_ln_g[i].reshape(1, -1), cmlp_ln_b[i].reshape(1, -1),
               w_spatial[i].astype(BF16), b_spatial[i].T)

    h1, xn2 = _outproj(o_f, o_b, p_lat, cm, x2, mod3, gla_norm_g[i].reshape(1, -1),
                       norm2_g[i].reshape(1, d), w_out[i].astype(BF16), seq)

    tabs = _route(xn2, peer_wq[i].T.astype(BF16), peer_sub_keys[i].astype(BF16))
    out = _peer(xn2, tabs, peer_u[i].astype(BF16), peer_v[i].T.astype(BF16), h1, mod3,
                final_norm_g.reshape(1, d), seq)
    return out.reshape(batch, seq, d)
```

```python
import functools

import jax
import jax.numpy as jnp
from jax import lax
from jax.experimental import pallas as pl
from jax.experimental.pallas import tpu as pltpu

F32 = jnp.float32
BF16 = jnp.bfloat16

D_MODEL = 2048
GLA_HEADS = 4
GLA_DK = 128
GLA_DV = 256
GLA_KEY_WIDTH = GLA_HEADS * GLA_DK
GLA_WIDTH = GLA_HEADS * GLA_DV
GLA_GATE_RANK = 16
GLA_GATE_TAU = 16.0
GLA_CHUNK = 64
CMLP_GROUPS = 8
CMLP_GROUP_DIM = 128
CMLP_WIDTH = CMLP_GROUPS * CMLP_GROUP_DIM
CMLP_CHUNK = 128
PEER_HEADS = 8
PEER_NKEYS = 128
PEER_EXPERTS = PEER_NKEYS * PEER_NKEYS
PEER_QDIM = 256
PEER_HALF = PEER_QDIM // 2
PEER_TOPK = 16
N_MOD = 6
EPS = 1e-6

MAIN_WIDTH = 2 * GLA_KEY_WIDTH + 2 * GLA_WIDTH + 2 * CMLP_WIDTH
LR_PAD = 128
MOD_ROWS = 8
MAX_TOKEN_TILE = 1024

VMEM_LIMIT_BYTES = 56 * 1024 * 1024

_NT = (((1,), (1,)), ((), ()))
_TN = (((0,), (0,)), ((), ()))


def _params(*sem):
    return pltpu.CompilerParams(dimension_semantics=sem, vmem_limit_bytes=VMEM_LIMIT_BYTES)


def _gelu(x):
    return 0.5 * x * (1.0 + lax.erf(x * (2.0 ** -0.5)))


def _silu(x):
    return x * jax.nn.sigmoid(x)


def _mod_kernel(cc_ref, w_ref, b_ref, o_ref):
    s = _silu(cc_ref[...])
    o_ref[...] = jnp.dot(s, w_ref[...], precision=lax.Precision.HIGHEST,
                         preferred_element_type=F32) + b_ref[...]


def _modulation(cc, w_mod, b_mod):
    n = w_mod.shape[1]
    tn = 1024
    return pl.pallas_call(
        _mod_kernel,
        grid=(n // tn,),
        in_specs=[pl.BlockSpec((MOD_ROWS, D_MODEL), lambda j: (0, 0)),
                  pl.BlockSpec((D_MODEL, tn), lambda j: (0, j)),
                  pl.BlockSpec((1, tn), lambda j: (0, j))],
        out_specs=pl.BlockSpec((MOD_ROWS, tn), lambda j: (0, j)),
        out_shape=jax.ShapeDtypeStruct((MOD_ROWS, n), F32),
        compiler_params=_params("arbitrary"),
        name="mod",
    )(cc, w_mod, b_mod.reshape(1, n))


def _inproj_kernel(x_ref, mod_ref, g_ref, w_ref, wlr_ref, p_ref, lr_ref, xn_ref):
    @pl.when(pl.program_id(1) == 0)
    def _():
        x = x_ref[...]
        y = x * lax.rsqrt(jnp.mean(x * x, axis=-1, keepdims=True) + EPS) * g_ref[...]
        xn = (y * (1.0 + mod_ref[0, 1:2, :]) + mod_ref[0, 0:1, :]).astype(BF16)
        xn_ref[...] = xn
        lr_ref[...] = jnp.dot(xn, wlr_ref[...], preferred_element_type=F32).astype(lr_ref.dtype)

    p_ref[...] = jnp.dot(xn_ref[...], w_ref[...], preferred_element_type=F32).astype(p_ref.dtype)


def _inproj(x2, mod3, norm_g, w_main, w_lr, mod_row_of_tile):
    t = x2.shape[0]
    tm = min(MAX_TOKEN_TILE, t)
    tn = 1024
    return pl.pallas_call(
        _inproj_kernel,
        grid=(t // tm, MAIN_WIDTH // tn),
        in_specs=[pl.BlockSpec((tm, D_MODEL), lambda i, j: (i, 0)),
                  pl.BlockSpec((1, N_MOD, D_MODEL), lambda i, j: (mod_row_of_tile(i, tm), 0, 0)),
                  pl.BlockSpec((1, D_MODEL), lambda i, j: (0, 0)),
                  pl.BlockSpec((D_MODEL, tn), lambda i, j: (0, j)),
                  pl.BlockSpec((D_MODEL, LR_PAD), lambda i, j: (0, 0))],
        out_specs=[pl.BlockSpec((tm, tn), lambda i, j: (i, j)),
                   pl.BlockSpec((tm, LR_PAD), lambda i, j: (i, 0))],
        out_shape=[jax.ShapeDtypeStruct((t, MAIN_WIDTH), BF16),
                   jax.ShapeDtypeStruct((t, LR_PAD), BF16)],
        scratch_shapes=[pltpu.VMEM((tm, D_MODEL), BF16)],
        compiler_params=_params("arbitrary", "arbitrary"),
        name="inproj",
    )(x2, mod3, norm_g, w_main, w_lr)


def _gla_kernel(qf_ref, kf_ref, vf_ref, lrf_ref, qb_ref, kb_ref, vb_ref, lrb_ref,
                wup_ref, bg_ref, s0_ref, of_ref, ob_ref, st_ref, *, n_chunks):
    @pl.when(pl.program_id(1) == 0)
    def _():
        st_ref[...] = s0_ref[...]

    c = GLA_CHUNK
    row = lax.broadcasted_iota(jnp.int32, (c, c), 0)
    col = lax.broadcasted_iota(jnp.int32, (c, c), 1)
    keep = (col <= row, col >= row)
    q_scale = GLA_DK ** -0.5
    dirs = ((qf_ref, kf_ref, vf_ref, lrf_ref, of_ref), (qb_ref, kb_ref, vb_ref, lrb_ref, ob_ref))

    for d, (q_ref, k_ref, v_ref, lr_ref, o_ref) in enumerate(dirs):
        order = range(n_chunks) if d == 0 else range(n_chunks - 1, -1, -1)
        chunk = lambda ci: slice(ci * c, (ci + 1) * c)
        z = jnp.dot(lr_ref[...], wup_ref[d], preferred_element_type=F32) + bg_ref[d]
        la = (jnp.minimum(z, 0.0) - jnp.log1p(jnp.exp(-jnp.abs(z)))) * (1.0 / GLA_GATE_TAU)
        tri = keep[d].astype(BF16)
        la_hi = la.astype(BF16)
        la_lo = (la - la_hi.astype(F32)).astype(BF16)
        cums = [jnp.dot(tri, la_hi[chunk(ci)], preferred_element_type=F32)
                + jnp.dot(tri, la_lo[chunk(ci)], preferred_element_type=F32)
                for ci in range(n_chunks)]
        tots = [cm[c - 1:c, :] if d == 0 else cm[0:1, :] for cm in cums]
        cum = jnp.concatenate(cums, axis=0)
        tot = jnp.concatenate([jnp.broadcast_to(t, (c, t.shape[1])) for t in tots], axis=0)
        kf = k_ref[...].astype(F32)
        qd = (q_ref[...].astype(F32) * q_scale * jnp.exp(cum)).astype(BF16)
        kd = (kf * jnp.exp(-cum)).astype(BF16)
        k2 = (kf * jnp.exp(tot - cum)).astype(BF16)
        etots = [jnp.exp(t) for t in tots]
        for h in range(GLA_HEADS):
            ks = slice(h * GLA_DK, (h + 1) * GLA_DK)
            vs = slice(h * GLA_DV, (h + 1) * GLA_DV)
            intra, incr = [], []
            for ci in range(n_chunks):
                rows = chunk(ci)
                vh = v_ref[rows, vs]
                sc = lax.dot_general(qd[rows, ks], kd[rows, ks], _NT, preferred_element_type=F32)
                sc = jnp.where(keep[d], sc, 0.0).astype(BF16)
                intra.append(jnp.dot(sc, vh, preferred_element_type=F32))
                incr.append(lax.dot_general(vh, k2[rows, ks], _TN, preferred_element_type=F32))
            st = st_ref[0, d, h]
            for ci in order:
                rows = chunk(ci)
                o_ref[rows, vs] = intra[ci] + lax.dot_general(qd[rows, ks], st.astype(BF16), _NT,
                                                              preferred_element_type=F32)
                st = st * etots[ci][:, ks] + incr[ci]
            st_ref[0, d, h] = st


def _gla(p, lr, w_up, b_gate, s0, batch, seq):
    tb = min(512, seq)
    nb = seq // tb
    qw, vw = GLA_KEY_WIDTH, GLA_WIDTH
    fwd = lambda col: (lambda b, s: (b * nb + s, col))
    bwd = lambda col: (lambda b, s: (b * nb + nb - 1 - s, col))
    state_spec = pl.BlockSpec((1, 2, GLA_HEADS, GLA_DV, GLA_DK), lambda b, s: (b, 0, 0, 0, 0))
    in_specs = []
    for m in (fwd, bwd):
        in_specs += [pl.BlockSpec((tb, qw), m(0)), pl.BlockSpec((tb, qw), m(1)),
                     pl.BlockSpec((tb, vw), m(1)), pl.BlockSpec((tb, LR_PAD), m(0))]
    in_specs += [pl.BlockSpec((2, LR_PAD, qw), lambda b, s: (0, 0, 0)),
                 pl.BlockSpec((2, 1, qw), lambda b, s: (0, 0, 0)),
                 state_spec]
    return pl.pallas_call(
        functools.partial(_gla_kernel, n_chunks=tb // GLA_CHUNK),
        grid=(batch, nb),
        in_specs=in_specs,
        out_specs=[pl.BlockSpec((tb, vw), fwd(0)), pl.BlockSpec((tb, vw), bwd(0)), state_spec],
        out_shape=[jax.ShapeDtypeStruct((batch * seq, vw), F32),
                   jax.ShapeDtypeStruct((batch * seq, vw), F32),
                   jax.ShapeDtypeStruct((batch, 2, GLA_HEADS, GLA_DV, GLA_DK), F32)],
        compiler_params=_params("arbitrary", "arbitrary"),
        name="gla",
    )(p, p, p, lr, p, p, p, lr, w_up, b_gate, s0)


def _cmlp_kernel(u_ref, v_ref, lng_ref, lnb_ref, ws_ref, bs_ref, o_ref, *, n_chunks):
    gv = _gelu(v_ref[...].astype(F32))
    mu = jnp.mean(gv, axis=-1, keepdims=True)
    dv = gv - mu
    var = jnp.mean(dv * dv, axis=-1, keepdims=True)
    vn = (dv * lax.rsqrt(var + EPS) * lng_ref[...] + lnb_ref[...]).astype(BF16)
    for ch in range(n_chunks):
        rows = slice(ch * CMLP_CHUNK, (ch + 1) * CMLP_CHUNK)
        for g in range(CMLP_GROUPS):
            cols = slice(g * CMLP_GROUP_DIM, (g + 1) * CMLP_GROUP_DIM)
            s = jnp.dot(ws_ref[g], vn[rows, cols], preferred_element_type=F32) + bs_ref[:, g:g + 1]
            o_ref[rows, cols] = (_gelu(u_ref[rows, cols].astype(F32)) * s).astype(o_ref.dtype)


def _cmlp(p, ln_g, ln_b, w_s, b_st):
    t = p.shape[0]
    tm = min(512, t)
    cu_blk = (2 * GLA_KEY_WIDTH + 2 * GLA_WIDTH) // CMLP_WIDTH
    return pl.pallas_call(
        functools.partial(_cmlp_kernel, n_chunks=tm // CMLP_CHUNK),
        grid=(t // tm,),
        in_specs=[pl.BlockSpec((tm, CMLP_WIDTH), lambda i: (i, cu_blk)),
                  pl.BlockSpec((tm, CMLP_WIDTH), lambda i: (i, cu_blk + 1)),
                  pl.BlockSpec((1, CMLP_WIDTH), lambda i: (0, 0)),
                  pl.BlockSpec((1, CMLP_WIDTH), lambda i: (0, 0)),
                  pl.BlockSpec((CMLP_GROUPS, CMLP_CHUNK, CMLP_CHUNK), lambda i: (0, 0, 0)),
                  pl.BlockSpec((CMLP_CHUNK, CMLP_GROUPS), lambda i: (0, 0))],
        out_specs=pl.BlockSpec((tm, CMLP_WIDTH), lambda i: (i, 0)),
        out_shape=jax.ShapeDtypeStruct((t, CMLP_WIDTH), BF16),
        compiler_params=_params("arbitrary"),
        name="cmlp",
    )(p, p, ln_g, ln_b, w_s, b_st)


def _outproj_kernel(of_ref, ob_ref, g_ref, cm_ref, x_ref, mod_ref, gng_ref, n2g_ref, w_ref,
                    h1_ref, xn2_ref):
    o = of_ref[...] + ob_ref[...]
    heads = []
    for h in range(GLA_HEADS):
        oh = o[:, h * GLA_DV:(h + 1) * GLA_DV]
        heads.append(oh * lax.rsqrt(jnp.mean(oh * oh, axis=-1, keepdims=True) + EPS))
    on = jnp.concatenate(heads, axis=-1) * gng_ref[...]
    y = (on * _silu(g_ref[...].astype(F32))).astype(BF16)
    mix = (jnp.dot(y, w_ref[0:GLA_WIDTH, :], preferred_element_type=F32)
           + jnp.dot(cm_ref[...], w_ref[GLA_WIDTH:GLA_WIDTH + CMLP_WIDTH, :], preferred_element_type=F32))
    h1 = x_ref[...] + mod_ref[0, 2:3, :] * mix
    h1_ref[...] = h1
    hn = h1 * lax.rsqrt(jnp.mean(h1 * h1, axis=-1, keepdims=True) + EPS) * n2g_ref[...]
    xn2_ref[...] = (hn * (1.0 + mod_ref[0, 4:5, :]) + mod_ref[0, 3:4, :]).astype(xn2_ref.dtype)


def _outproj(o_f, o_b, p, cm, x2, mod3, gla_norm_g, norm2_g, w_out, seq):
    t = x2.shape[0]
    tm = min(512, t)
    g_blk = (2 * GLA_KEY_WIDTH + GLA_WIDTH) // GLA_WIDTH
    return pl.pallas_call(
        _outproj_kernel,
        grid=(t // tm,),
        in_specs=[pl.BlockSpec((tm, GLA_WIDTH), lambda i: (i, 0)),
                  pl.BlockSpec((tm, GLA_WIDTH), lambda i: (i, 0)),
                  pl.BlockSpec((tm, GLA_WIDTH), lambda i: (i, g_blk)),
                  pl.BlockSpec((tm, CMLP_WIDTH), lambda i: (i, 0)),
                  pl.BlockSpec((tm, D_MODEL), lambda i: (i, 0)),
                  pl.BlockSpec((1, N_MOD, D_MODEL), lambda i: ((i * tm) // seq, 0, 0)),
                  pl.BlockSpec((1, GLA_WIDTH), lambda i: (0, 0)),
                  pl.BlockSpec((1, D_MODEL), lambda i: (0, 0)),
                  pl.BlockSpec((GLA_WIDTH + CMLP_WIDTH, D_MODEL), lambda i: (0, 0))],
        out_specs=[pl.BlockSpec((tm, D_MODEL), lambda i: (i, 0)),
                   pl.BlockSpec((tm, D_MODEL), lambda i: (i, 0))],
        out_shape=[jax.ShapeDtypeStruct((t, D_MODEL), F32),
                   jax.ShapeDtypeStruct((t, D_MODEL), BF16)],
        compiler_params=_params("arbitrary"),
        name="outproj",
    )(o_f, o_b, p, cm, x2, mod3, gla_norm_g, norm2_g, w_out)


def _top16(s, key_iota, exact, want_rank=True):
    n = s.shape[1]
    neg = jnp.float32(-jnp.inf)
    vals = []
    rank = jnp.full(s.shape, float(PEER_TOPK), F32) if want_rank else None
    for r in range(PEER_TOPK):
        m = jnp.max(s, axis=0, keepdims=True)
        sel = s == m
        if exact:
            idx = jnp.min(jnp.where(sel, key_iota, float(s.shape[0])), axis=0, keepdims=True)
            sel = key_iota == idx
        if want_rank:
            rank = jnp.where(sel, float(r), rank)
        s = jnp.where(sel, neg, s)
        vals.append(m)
    taken = jnp.sum(jnp.where(s == neg, 1.0, 0.0), axis=0, keepdims=True)
    return jnp.concatenate(vals, axis=0), rank, taken


_CAND_INVALID = 1 << 20
_CAND_ROWS = 72


def _staircase_positions():
    assert PEER_TOPK == 16
    k, inv = PEER_TOPK, _CAND_INVALID
    pos = list(range(k))
    pos += [inv] + [a * k for a in range(1, k)]
    pos += [inv] + [k + b for b in range(1, 8)]
    pos += [inv, inv] + [a * k + 1 for a in range(2, 8)]
    pos += [2 * k + b if b in (2, 3, 4) else inv for b in range(8)]
    pos += [3 * k + b if b in (2, 3) else inv for b in range(8)]
    pos += [4 * k + b if b == 2 else inv for b in range(8)]
    assert len(pos) == _CAND_ROWS and sum(p != inv for p in pos) == 50
    return jnp.broadcast_to(jnp.array(pos, F32)[:, None], (_CAND_ROWS, ROUTE_LANES))


ROUTE_LANES = 128
ROUTE_TILES_PER_ITER = 4


def _route_kernel(xn_ref, wqt_ref, sk_ref, pos_ref, e1_ref, c1_ref, e2_ref, r2_ref, qt_ref):
    qt_ref[...] = lax.dot_general(wqt_ref[...], xn_ref[...], _NT, preferred_element_type=F32)
    n = ROUTE_LANES
    n_tiles = xn_ref.shape[0] // n
    k = PEER_TOPK
    key_iota = lax.broadcasted_iota(jnp.int32, (PEER_NKEYS, n), 0).astype(F32)
    row16 = lax.broadcasted_iota(jnp.int32, (k, n), 0)
    neg = jnp.float32(-jnp.inf)
    invalid = float(_CAND_INVALID)

    def tiles(t, carry):
        first = t * ROUTE_TILES_PER_ITER
        taken = [tile(first + u, exact=False) for u in range(ROUTE_TILES_PER_ITER)]

        @pl.when(jnp.max(functools.reduce(jnp.maximum, taken)) > float(k))
        def _():
            for u in range(ROUTE_TILES_PER_ITER):
                tile(first + u, exact=True)
        return carry

    def tile(t, exact):
        h = t // n_tiles
        lanes = pl.ds(pl.multiple_of((t % n_tiles) * n, n), n)
        base = pl.multiple_of(h * PEER_QDIM, PEER_QDIM)
        q1 = qt_ref[pl.ds(base, PEER_HALF), lanes].astype(BF16)
        q2 = qt_ref[pl.ds(base + PEER_HALF, PEER_HALF), lanes].astype(BF16)
        s1 = jnp.dot(sk_ref[0, h], q1, preferred_element_type=F32)
        s2 = jnp.dot(sk_ref[1, h], q2, preferred_element_type=F32)
        v1, rank1, taken1 = _top16(s1, key_iota, exact, want_rank=exact)
        v2, rank2, taken2 = _top16(s2, key_iota, exact)
        pos = pos_ref[...]
        cand = jnp.concatenate([v1[0:1] + v2, v1 + v2[0:1], v1[1:2] + v2[0:8], v1[0:8] + v2[1:2],
                                v1[2:3] + v2[0:8], v1[3:4] + v2[0:8], v1[4:5] + v2[0:8]], axis=0)
        cand = jnp.where(pos < invalid, cand, neg)
        best = cand[0:1, :]
        rem = cand
        for _ in range(k):
            m = jnp.max(rem, axis=0, keepdims=True)
            sel = rem == m
            if exact:
                idx = jnp.min(jnp.where(sel, pos, invalid), axis=0, keepdims=True)
                sel = pos == idx
            rem = jnp.where(sel, neg, rem)
        chosen = rem != cand
        z = jnp.sum(jnp.where(chosen, jnp.exp(cand - best), 0.0), axis=0, keepdims=True)
        ch = chosen.astype(F32)
        rowsum = lambda lo, hi: jnp.sum(ch[lo:hi], axis=0, keepdims=True)
        cnt = ch[16:32] + jnp.concatenate([ch[40:48], jnp.zeros((8, n), F32)], axis=0)
        for a, (lo, hi) in enumerate(((0, 16), (32, 40), (48, 56), (56, 64), (64, 72))):
            cnt = cnt + jnp.where(row16 == a, rowsum(lo, hi), 0.0)
        c1 = jnp.zeros((PEER_NKEYS, n), F32)
        for a in range(k):
            is_rank_a = (rank1 == float(a)) if exact else (s1 == v1[a:a + 1])
            c1 = jnp.where(is_rank_a, cnt[a:a + 1], c1)
        e1_ref[h, :, lanes] = jnp.exp(s1 - v1[0:1, :]) * (1.0 / z)
        c1_ref[h, :, lanes] = c1
        e2_ref[h, :, lanes] = jnp.exp(s2 - v2[0:1, :]).astype(e2_ref.dtype)
        r2_ref[h, :, lanes] = rank2.astype(r2_ref.dtype)
        taken_pairs = jnp.sum(ch, axis=0, keepdims=True)
        return jnp.maximum(jnp.maximum(taken1, taken2), taken_pairs)

    lax.fori_loop(0, PEER_HEADS * n_tiles // ROUTE_TILES_PER_ITER, tiles, 0)


def _route(xn2, wq_t, sk):
    t = xn2.shape[0]
    tr = min(512, t)
    tab = lambda dt: jax.ShapeDtypeStruct((PEER_HEADS, PEER_NKEYS, t), dt)
    tab_spec = pl.BlockSpec((PEER_HEADS, PEER_NKEYS, tr), lambda i: (0, 0, i))
    return pl.pallas_call(
        _route_kernel,
        grid=(t // tr,),
        in_specs=[pl.BlockSpec((tr, D_MODEL), lambda i: (i, 0)),
                  pl.BlockSpec((PEER_HEADS * PEER_QDIM, D_MODEL), lambda i: (0, 0)),
                  pl.BlockSpec((2, PEER_HEADS, PEER_NKEYS, PEER_HALF), lambda i: (0, 0, 0, 0)),
                  pl.BlockSpec((_CAND_ROWS, ROUTE_LANES), lambda i: (0, 0))],
        out_specs=[tab_spec] * 4,
        out_shape=[tab(F32), tab(F32), tab(BF16), tab(BF16)],
        scratch_shapes=[pltpu.VMEM((PEER_HEADS * PEER_QDIM, tr), F32)],
        compiler_params=_params("arbitrary"),
        name="route",
    )(xn2, wq_t, sk, _staircase_positions())


PEER_CHUNK = 512
PEER_CHUNKS_PER_STEP = 2
PEER_TOKENS = 512
PEER_WEIGHT_LANES = 256
BF16_ROWS = 16
PEER_VMEM_LIMIT_BYTES = 63 * 1024 * 1024


def _peer_kernel(xn_ref, e1_ref, c1_ref, e2_ref, r2_ref, u_ref, vt_ref, h1_ref, mod_ref, fg_ref,
                 o_ref, acc_ref, xs_ref):
    j = pl.program_id(1)
    ec = PEER_CHUNK
    n = xn_ref.shape[0]
    lanes = PEER_WEIGHT_LANES
    rows_per_chunk = ec // PEER_NKEYS
    rows_per_step = PEER_CHUNKS_PER_STEP * rows_per_chunk

    @pl.when(j == 0)
    def _():
        acc_ref[...] = jnp.zeros_like(acc_ref)
        xs_ref[...] = xn_ref[...]

    zero = jnp.zeros((), BF16)
    keys = pl.ds(pl.multiple_of(j * rows_per_step, rows_per_step), rows_per_step)
    chains = [(kk, tc) for kk in range(PEER_CHUNKS_PER_STEP) for tc in range(n // lanes)]

    def routing_weights(kk, tc):
        cols = slice(tc * lanes, (tc + 1) * lanes)
        row = (BF16_ROWS, lanes)
        reps = PEER_NKEYS // BF16_ROWS
        blocks = []
        for r in range(rows_per_chunk):
            i1 = kk * rows_per_chunk + r
            w = None
            for h in range(PEER_HEADS):
                e1 = jnp.broadcast_to(e1_ref[h, keys, cols][i1:i1 + 1], row).astype(BF16)
                c1 = jnp.broadcast_to(c1_ref[h, keys, cols][i1:i1 + 1], row).astype(BF16)
                e1 = jnp.concatenate([e1] * reps, axis=0)
                c1 = jnp.concatenate([c1] * reps, axis=0)
                t = e1 * jnp.where(r2_ref[h, :, cols] < c1, e2_ref[h, :, cols], zero)
                w = t if w is None else w + t
            blocks.append(w)
        return jnp.concatenate(blocks, axis=0)

    acts = [lax.dot_general(u_ref[kk * ec:(kk + 1) * ec, :], xs_ref[tc * lanes:(tc + 1) * lanes, :], _NT,
                            preferred_element_type=F32) for kk, tc in chains]
    for (kk, tc), act in zip(chains, acts):
        p = routing_weights(kk, tc) * _gelu(act).astype(BF16)
        acc_ref[:, tc * lanes:(tc + 1) * lanes] += jnp.dot(vt_ref[:, kk * ec:(kk + 1) * ec], p,
                                                           preferred_element_type=F32)

    @pl.when(j == pl.num_programs(1) - 1)
    def _():
        h2 = h1_ref[...] + mod_ref[0, 5:6, :] * acc_ref[...].T
        o_ref[...] = h2 * lax.rsqrt(jnp.mean(h2 * h2, axis=-1, keepdims=True) + EPS) * fg_ref[...]


def _peer(xn2, tabs, u, v_t, h1, mod3, final_g, seq):
    t = xn2.shape[0]
    tb = min(PEER_TOKENS, t)
    ecs = PEER_CHUNKS_PER_STEP * PEER_CHUNK
    tab_spec = pl.BlockSpec((PEER_HEADS, PEER_NKEYS, tb), lambda i, j: (0, 0, i))
    return pl.pallas_call(
        _peer_kernel,
        grid=(t // tb, PEER_EXPERTS // ecs),
        in_specs=[pl.BlockSpec((tb, D_MODEL), lambda i, j: (i, 0)),
                  tab_spec, tab_spec, tab_spec, tab_spec,
                  pl.BlockSpec((ecs, D_MODEL), lambda i, j: (j, 0)),
                  pl.BlockSpec((D_MODEL, ecs), lambda i, j: (0, j)),
                  pl.BlockSpec((tb, D_MODEL), lambda i, j: (i, 0)),
                  pl.BlockSpec((1, N_MOD, D_MODEL), lambda i, j: ((i * tb) // seq, 0, 0)),
                  pl.BlockSpec((1, D_MODEL), lambda i, j: (0, 0))],
        out_specs=pl.BlockSpec((tb, D_MODEL), lambda i, j: (i, 0)),
        out_shape=jax.ShapeDtypeStruct((t, D_MODEL), F32),
        scratch_shapes=[pltpu.VMEM((D_MODEL, tb), F32), pltpu.VMEM((tb, D_MODEL), BF16)],
        compiler_params=pltpu.CompilerParams(dimension_semantics=("arbitrary", "arbitrary"),
                                             vmem_limit_bytes=PEER_VMEM_LIMIT_BYTES),
        name="peer",
    )(xn2, *tabs, u, v_t, h1, mod3, final_g)


def kernel(x, c, ctx, c_ctx, norm1_g, norm2_g, w_mod, b_mod, w_in, w_gate_up, b_gate, gla_norm_g,
           cmlp_ln_g, cmlp_ln_b, w_spatial, b_spatial, w_out, peer_wq, peer_sub_keys, peer_u, peer_v,
           final_norm_g):
    batch, seq, d = x.shape
    ctx_len = ctx.shape[1]
    depth = w_mod.shape[0]
    assert d == D_MODEL and depth == 1 and batch + 1 <= MOD_ROWS
    assert seq % MAX_TOKEN_TILE == 0 and ctx_len % GLA_CHUNK == 0
    i = 0

    cc = jnp.zeros((MOD_ROWS, d), F32).at[:batch].set(c).at[batch].set(c_ctx)
    mod3 = _modulation(cc, w_mod[i], b_mod[i]).reshape(MOD_ROWS, N_MOD, d)

    qk_end = 2 * GLA_KEY_WIDTH + 2 * GLA_WIDTH
    lr_end = qk_end + 2 * GLA_GATE_RANK
    w = w_in[i].astype(BF16)
    w_main = jnp.concatenate([w[:, :qk_end], w[:, lr_end:]], axis=1)
    w_lr = jnp.pad(w[:, qk_end:lr_end], ((0, 0), (0, LR_PAD - 2 * GLA_GATE_RANK)))
    n1 = norm1_g[i].reshape(1, d)
    x2 = x.reshape(batch * seq, d)
    p_lat, lr_lat = _inproj(x2, mod3, n1, w_main, w_lr, lambda t, tm: (t * tm) // seq)
    p_ctx, lr_ctx = _inproj(ctx.reshape(batch * ctx_len, d), mod3, n1, w_main, w_lr, lambda t, tm: batch)

    w_up = jnp.zeros((2, LR_PAD, GLA_KEY_WIDTH), F32)
    for dd in range(2):
        w_up = w_up.at[dd, dd * GLA_GATE_RANK:(dd + 1) * GLA_GATE_RANK].set(w_gate_up[i, dd])
    w_up = w_up.astype(BF16)
    bg = b_gate[i].reshape(2, 1, GLA_KEY_WIDTH)
    s_zero = jnp.zeros((batch, 2, GLA_HEADS, GLA_DV, GLA_DK), F32)
    _, _, s_ctx = _gla(p_ctx, lr_ctx, w_up, bg, s_zero, batch, ctx_len)
    o_f, o_b, _ = _gla(p_lat, lr_lat, w_up, bg, s_ctx, batch, seq)

    cm = _cmlp(p_lat, cmlp_ln_g[i].reshape(1, -1), cmlp_ln_b[i].reshape(1, -1),
               w_spatial[i].astype(BF16), b_spatial[i].T)

    h1, xn2 = _outproj(o_f, o_b, p_lat, cm, x2, mod3, gla_norm_g[i].reshape(1, -1),
                       norm2_g[i].reshape(1, d), w_out[i].astype(BF16), seq)

    tabs = _route(xn2, peer_wq[i].T.astype(BF16), peer_sub_keys[i].astype(BF16))
    out = _peer(xn2, tabs, peer_u[i].astype(BF16), peer_v[i].T.astype(BF16), h1, mod3,
                final_norm_g.reshape(1, d), seq)
    return out.reshape(batch, seq, d)
```

```python
import functools

import jax
import jax.numpy as jnp
from jax import lax
from jax.experimental import pallas as pl
from jax.experimental.pallas import tpu as pltpu

F32 = jnp.float32
BF16 = jnp.bfloat16

D_MODEL = 2048
GLA_HEADS = 4
GLA_DK = 128
GLA_DV = 256
GLA_KEY_WIDTH = GLA_HEADS * GLA_DK
GLA_WIDTH = GLA_HEADS * GLA_DV
GLA_GATE_RANK = 16
GLA_GATE_TAU = 16.0
GLA_CHUNK = 64
CMLP_GROUPS = 8
CMLP_GROUP_DIM = 128
CMLP_WIDTH = CMLP_GROUPS * CMLP_GROUP_DIM
CMLP_CHUNK = 128
PEER_HEADS = 8
PEER_NKEYS = 128
PEER_EXPERTS = PEER_NKEYS * PEER_NKEYS
PEER_QDIM = 256
PEER_HALF = PEER_QDIM // 2
PEER_TOPK = 16
N_MOD = 6
EPS = 1e-6

MAIN_WIDTH = 2 * GLA_KEY_WIDTH + 2 * GLA_WIDTH + 2 * CMLP_WIDTH
LR_PAD = 128
MOD_ROWS = 8
MAX_TOKEN_TILE = 1024

VMEM_LIMIT_BYTES = 56 * 1024 * 1024

_NT = (((1,), (1,)), ((), ()))
_TN = (((0,), (0,)), ((), ()))


def _params(*sem):
    return pltpu.CompilerParams(dimension_semantics=sem, vmem_limit_bytes=VMEM_LIMIT_BYTES)


def _gelu(x):
    return 0.5 * x * (1.0 + lax.erf(x * (2.0 ** -0.5)))


def _silu(x):
    return x * jax.nn.sigmoid(x)


def _mod_kernel(cc_ref, w_ref, b_ref, o_ref):
    s = _silu(cc_ref[...])
    o_ref[...] = jnp.dot(s, w_ref[...], precision=lax.Precision.HIGHEST,
                         preferred_element_type=F32) + b_ref[...]


def _modulation(cc, w_mod, b_mod):
    n = w_mod.shape[1]
    tn = 1024
    return pl.pallas_call(
        _mod_kernel,
        grid=(n // tn,),
        in_specs=[pl.BlockSpec((MOD_ROWS, D_MODEL), lambda j: (0, 0)),
                  pl.BlockSpec((D_MODEL, tn), lambda j: (0, j)),
                  pl.BlockSpec((1, tn), lambda j: (0, j))],
        out_specs=pl.BlockSpec((MOD_ROWS, tn), lambda j: (0, j)),
        out_shape=jax.ShapeDtypeStruct((MOD_ROWS, n), F32),
        compiler_params=_params("arbitrary"),
        name="mod",
    )(cc, w_mod, b_mod.reshape(1, n))


def _inproj_kernel(x_ref, mod_ref, g_ref, w_ref, wlr_ref, p_ref, lr_ref, xn_ref):
    @pl.when(pl.program_id(1) == 0)
    def _():
        x = x_ref[...]
        y = x * lax.rsqrt(jnp.mean(x * x, axis=-1, keepdims=True) + EPS) * g_ref[...]
        xn = (y * (1.0 + mod_ref[0, 1:2, :]) + mod_ref[0, 0:1, :]).astype(BF16)
        xn_ref[...] = xn
        lr_ref[...] = jnp.dot(xn, wlr_ref[...], preferred_element_type=F32).astype(lr_ref.dtype)

    p_ref[...] = jnp.dot(xn_ref[...], w_ref[...], preferred_element_type=F32).astype(p_ref.dtype)


def _inproj(x2, mod3, norm_g, w_main, w_lr, mod_row_of_tile):
    t = x2.shape[0]
    tm = min(MAX_TOKEN_TILE, t)
    tn = 1024
    return pl.pallas_call(
        _inproj_kernel,
        grid=(t // tm, MAIN_WIDTH // tn),
        in_specs=[pl.BlockSpec((tm, D_MODEL), lambda i, j: (i, 0)),
                  pl.BlockSpec((1, N_MOD, D_MODEL), lambda i, j: (mod_row_of_tile(i, tm), 0, 0)),
                  pl.BlockSpec((1, D_MODEL), lambda i, j: (0, 0)),
                  pl.BlockSpec((D_MODEL, tn), lambda i, j: (0, j)),
                  pl.BlockSpec((D_MODEL, LR_PAD), lambda i, j: (0, 0))],
        out_specs=[pl.BlockSpec((tm, tn), lambda i, j: (i, j)),
                   pl.BlockSpec((tm, LR_PAD), lambda i, j: (i, 0))],
        out_shape=[jax.ShapeDtypeStruct((t, MAIN_WIDTH), BF16),
                   jax.ShapeDtypeStruct((t, LR_PAD), BF16)],
        scratch_shapes=[pltpu.VMEM((tm, D_MODEL), BF16)],
        compiler_params=_params("arbitrary", "arbitrary"),
        name="inproj",
    )(x2, mod3, norm_g, w_main, w_lr)


def _gla_kernel(qf_ref, kf_ref, vf_ref, lrf_ref, qb_ref, kb_ref, vb_ref, lrb_ref,
                wup_ref, bg_ref, s0_ref, of_ref, ob_ref, st_ref, *, n_chunks):
    @pl.when(pl.program_id(1) == 0)
    def _():
        st_ref[...] = s0_ref[...]

    c = GLA_CHUNK
    row = lax.broadcasted_iota(jnp.int32, (c, c), 0)
    col = lax.broadcasted_iota(jnp.int32, (c, c), 1)
    keep = (col <= row, col >= row)
    q_scale = GLA_DK ** -0.5
    dirs = ((qf_ref, kf_ref, vf_ref, lrf_ref, of_ref), (qb_ref, kb_ref, vb_ref, lrb_ref, ob_ref))

    for d, (q_ref, k_ref, v_ref, lr_ref, o_ref) in enumerate(dirs):
        order = range(n_chunks) if d == 0 else range(n_chunks - 1, -1, -1)
        chunk = lambda ci: slice(ci * c, (ci + 1) * c)
        z = jnp.dot(lr_ref[...], wup_ref[d], preferred_element_type=F32) + bg_ref[d]
        la = (jnp.minimum(z, 0.0) - jnp.log1p(jnp.exp(-jnp.abs(z)))) * (1.0 / GLA_GATE_TAU)
        tri = keep[d].astype(BF16)
        la_hi = la.astype(BF16)
        la_lo = (la - la_hi.astype(F32)).astype(BF16)
        cums = [jnp.dot(tri, la_hi[chunk(ci)], preferred_element_type=F32)
                + jnp.dot(tri, la_lo[chunk(ci)], preferred_element_type=F32)
                for ci in range(n_chunks)]
        tots = [cm[c - 1:c, :] if d == 0 else cm[0:1, :] for cm in cums]
        cum = jnp.concatenate(cums, axis=0)
        tot = jnp.concatenate([jnp.broadcast_to(t, (c, t.shape[1])) for t in tots], axis=0)
        kf = k_ref[...].astype(F32)
        qd = (q_ref[...].astype(F32) * q_scale * jnp.exp(cum)).astype(BF16)
        kd = (kf * jnp.exp(-cum)).astype(BF16)
        k2 = (kf * jnp.exp(tot - cum)).astype(BF16)
        etots = [jnp.exp(t) for t in tots]
        for h in range(GLA_HEADS):
            ks = slice(h * GLA_DK, (h + 1) * GLA_DK)
            vs = slice(h * GLA_DV, (h + 1) * GLA_DV)
            intra, incr = [], []
            for ci in range(n_chunks):
                rows = chunk(ci)
                vh = v_ref[rows, vs]
                sc = lax.dot_general(qd[rows, ks], kd[rows, ks], _NT, preferred_element_type=F32)
                sc = jnp.where(keep[d], sc, 0.0).astype(BF16)
                intra.append(jnp.dot(sc, vh, preferred_element_type=F32))
                incr.append(lax.dot_general(vh, k2[rows, ks], _TN, preferred_element_type=F32))
            st = st_ref[0, d, h]
            for ci in order:
                rows = chunk(ci)
                o_ref[rows, vs] = intra[ci] + lax.dot_general(qd[rows, ks], st.astype(BF16), _NT,
                                                              preferred_element_type=F32)
                st = st * etots[ci][:, ks] + incr[ci]
            st_ref[0, d, h] = st


def _gla(p, lr, w_up, b_gate, s0, batch, seq):
    tb = min(512, seq)
    nb = seq // tb
    qw, vw = GLA_KEY_WIDTH, GLA_WIDTH
    fwd = lambda col: (lambda b, s: (b * nb + s, col))
    bwd = lambda col: (lambda b, s: (b * nb + nb - 1 - s, col))
    state_spec = pl.BlockSpec((1, 2, GLA_HEADS, GLA_DV, GLA_DK), lambda b, s: (b, 0, 0, 0, 0))
    in_specs = []
    for m in (fwd, bwd):
        in_specs += [pl.BlockSpec((tb, qw), m(0)), pl.BlockSpec((tb, qw), m(1)),
                     pl.BlockSpec((tb, vw), m(1)), pl.BlockSpec((tb, LR_PAD), m(0))]
    in_specs += [pl.BlockSpec((2, LR_PAD, qw), lambda b, s: (0, 0, 0)),
                 pl.BlockSpec((2, 1, qw), lambda b, s: (0, 0, 0)),
                 state_spec]
    return pl.pallas_call(
        functools.partial(_gla_kernel, n_chunks=tb // GLA_CHUNK),
        grid=(batch, nb),
        in_specs=in_specs,
        out_specs=[pl.BlockSpec((tb, vw), fwd(0)), pl.BlockSpec((tb, vw), bwd(0)), state_spec],
        out_shape=[jax.ShapeDtypeStruct((batch * seq, vw), F32),
                   jax.ShapeDtypeStruct((batch * seq, vw), F32),
                   jax.ShapeDtypeStruct((batch, 2, GLA_HEADS, GLA_DV, GLA_DK), F32)],
        compiler_params=_params("arbitrary", "arbitrary"),
        name="gla",
    )(p, p, p, lr, p, p, p, lr, w_up, b_gate, s0)


def _cmlp_kernel(u_ref, v_ref, lng_ref, lnb_ref, ws_ref, bs_ref, o_ref, *, n_chunks):
    gv = _gelu(v_ref[...].astype(F32))
    mu = jnp.mean(gv, axis=-1, keepdims=True)
    dv = gv - mu
    var = jnp.mean(dv * dv, axis=-1, keepdims=True)
    vn = (dv * lax.rsqrt(var + EPS) * lng_ref[...] + lnb_ref[...]).astype(BF16)
    for ch in range(n_chunks):
        rows = slice(ch * CMLP_CHUNK, (ch + 1) * CMLP_CHUNK)
        for g in range(CMLP_GROUPS):
            cols = slice(g * CMLP_GROUP_DIM, (g + 1) * CMLP_GROUP_DIM)
            s = jnp.dot(ws_ref[g], vn[rows, cols], preferred_element_type=F32) + bs_ref[:, g:g + 1]
            o_ref[rows, cols] = (_gelu(u_ref[rows, cols].astype(F32)) * s).astype(o_ref.dtype)


def _cmlp(p, ln_g, ln_b, w_s, b_st):
    t = p.shape[0]
    tm = min(512, t)
    cu_blk = (2 * GLA_KEY_WIDTH + 2 * GLA_WIDTH) // CMLP_WIDTH
    return pl.pallas_call(
        functools.partial(_cmlp_kernel, n_chunks=tm // CMLP_CHUNK),
        grid=(t // tm,),
        in_specs=[pl.BlockSpec((tm, CMLP_WIDTH), lambda i: (i, cu_blk)),
                  pl.BlockSpec((tm, CMLP_WIDTH), lambda i: (i, cu_blk + 1)),
                  pl.BlockSpec((1, CMLP_WIDTH), lambda i: (0, 0)),
                  pl.BlockSpec((1, CMLP_WIDTH), lambda i: (0, 0)),
                  pl.BlockSpec((CMLP_GROUPS, CMLP_CHUNK, CMLP_CHUNK), lambda i: (0, 0, 0)),
                  pl.BlockSpec((CMLP_CHUNK, CMLP_GROUPS), lambda i: (0, 0))],
        out_specs=pl.BlockSpec((tm, CMLP_WIDTH), lambda i: (i, 0)),
        out_shape=jax.ShapeDtypeStruct((t, CMLP_WIDTH), BF16),
        compiler_params=_params("arbitrary"),
        name="cmlp",
    )(p, p, ln_g, ln_b, w_s, b_st)


def _outproj_kernel(of_ref, ob_ref, g_ref, cm_ref, x_ref, mod_ref, gng_ref, n2g_ref, w_ref,
                    h1_ref, xn2_ref):
    o = of_ref[...] + ob_ref[...]
    heads = []
    for h in range(GLA_HEADS):
        oh = o[:, h * GLA_DV:(h + 1) * GLA_DV]
        heads.append(oh * lax.rsqrt(jnp.mean(oh * oh, axis=-1, keepdims=True) + EPS))
    on = jnp.concatenate(heads, axis=-1) * gng_ref[...]
    y = (on * _silu(g_ref[...].astype(F32))).astype(BF16)
    mix = (jnp.dot(y, w_ref[0:GLA_WIDTH, :], preferred_element_type=F32)
           + jnp.dot(cm_ref[...], w_ref[GLA_WIDTH:GLA_WIDTH + CMLP_WIDTH, :], preferred_element_type=F32))
    h1 = x_ref[...] + mod_ref[0, 2:3, :] * mix
    h1_ref[...] = h1
    hn = h1 * lax.rsqrt(jnp.mean(h1 * h1, axis=-1, keepdims=True) + EPS) * n2g_ref[...]
    xn2_ref[...] = (hn * (1.0 + mod_ref[0, 4:5, :]) + mod_ref[0, 3:4, :]).astype(xn2_ref.dtype)


def _outproj(o_f, o_b, p, cm, x2, mod3, gla_norm_g, norm2_g, w_out, seq):
    t = x2.shape[0]
    tm = min(512, t)
    g_blk = (2 * GLA_KEY_WIDTH + GLA_WIDTH) // GLA_WIDTH
    return pl.pallas_call(
        _outproj_kernel,
        grid=(t // tm,),
        in_specs=[pl.BlockSpec((tm, GLA_WIDTH), lambda i: (i, 0)),
                  pl.BlockSpec((tm, GLA_WIDTH), lambda i: (i, 0)),
                  pl.BlockSpec((tm, GLA_WIDTH), lambda i: (i, g_blk)),
                  pl.BlockSpec((tm, CMLP_WIDTH), lambda i: (i, 0)),
                  pl.BlockSpec((tm, D_MODEL), lambda i: (i, 0)),
                  pl.BlockSpec((1, N_MOD, D_MODEL), lambda i: ((i * tm) // seq, 0, 0)),
                  pl.BlockSpec((1, GLA_WIDTH), lambda i: (0, 0)),
                  pl.BlockSpec((1, D_MODEL), lambda i: (0, 0)),
                  pl.BlockSpec((GLA_WIDTH + CMLP_WIDTH, D_MODEL), lambda i: (0, 0))],
        out_specs=[pl.BlockSpec((tm, D_MODEL), lambda i: (i, 0)),
                   pl.BlockSpec((tm, D_MODEL), lambda i: (i, 0))],
        out_shape=[jax.ShapeDtypeStruct((t, D_MODEL), F32),
                   jax.ShapeDtypeStruct((t, D_MODEL), BF16)],
        compiler_params=_params("arbitrary"),
        name="outproj",
    )(o_f, o_b, p, cm, x2, mod3, gla_norm_g, norm2_g, w_out)


def _top16(s, key_iota, exact):
    neg = jnp.float32(-jnp.inf)
    vals = []
    rank = jnp.full(s.shape, float(PEER_TOPK), F32)
    for r in range(PEER_TOPK):
        m = jnp.max(s, axis=0, keepdims=True)
        sel = s == m
        if exact:
            idx = jnp.min(jnp.where(sel, key_iota, float(s.shape[0])), axis=0, keepdims=True)
            sel = key_iota == idx
        rank = jnp.where(sel, float(r), rank)
        s = jnp.where(sel, neg, s)
        vals.append(m)
    taken = jnp.sum(jnp.where(s == neg, 1.0, 0.0), axis=0, keepdims=True)
    return jnp.concatenate(vals, axis=0), rank, taken


SUBLANES = 8


def _key_groups(s):
    return [s[q * SUBLANES:(q + 1) * SUBLANES] for q in range(s.shape[0] // SUBLANES)]


def _oddeven_mergesort_pairs(n):
    pairs = []

    def merge(lo, hi, r):
        step = r * 2
        if step < hi - lo:
            merge(lo, hi, step)
            merge(lo + r, hi, step)
            pairs.extend((i, i + r) for i in range(lo + r, hi - r, step))
        else:
            pairs.append((lo, lo + r))

    def sort(lo, hi):
        if hi - lo >= 1:
            mid = lo + (hi - lo) // 2
            sort(lo, mid)
            sort(mid + 1, hi)
            merge(lo, hi, 1)

    sort(0, n - 1)
    return pairs


def _sorted_top16(s):
    k = PEER_TOPK
    rows = _key_groups(s)
    assert len(rows) == k

    def exchange(i, j):
        rows[i], rows[j] = jnp.maximum(rows[i], rows[j]), jnp.minimum(rows[i], rows[j])

    for i, j in _oddeven_mergesort_pairs(k):
        exchange(i, j)
    shift = SUBLANES // 2
    while shift:
        other = [pltpu.roll(r, shift, 0) for r in rows]
        rows = [jnp.maximum(rows[i], other[k - 1 - i]) for i in range(k)]
        d = k // 2
        while d:
            for i in range(k):
                if not i & d:
                    exchange(i, i + d)
            d //= 2
        shift //= 2
    reach = sum(jnp.where(g >= rows[k - 1], 1.0, 0.0) for g in _key_groups(s))
    repeats = sum(jnp.where(rows[b] == rows[b + 1], 1.0, 0.0) for b in range(k - 1))
    taken = jnp.sum(reach, axis=0, keepdims=True) + float(k) * repeats[0:1]
    return rows, taken


_CAND_INVALID = 1 << 20
_CAND_ROWS = 72


def _staircase_positions():
    assert PEER_TOPK == 16
    k, inv = PEER_TOPK, _CAND_INVALID
    pos = list(range(k))
    pos += [inv] + [a * k for a in range(1, k)]
    pos += [inv] + [k + b for b in range(1, 8)]
    pos += [inv, inv] + [a * k + 1 for a in range(2, 8)]
    pos += [2 * k + b if b in (2, 3, 4) else inv for b in range(8)]
    pos += [3 * k + b if b in (2, 3) else inv for b in range(8)]
    pos += [4 * k + b if b == 2 else inv for b in range(8)]
    assert len(pos) == _CAND_ROWS and sum(p != inv for p in pos) == 50
    return jnp.broadcast_to(jnp.array(pos, F32)[:, None], (_CAND_ROWS, ROUTE_LANES))


ROUTE_LANES = 128
ROUTE_TILES_PER_ITER = 4


def _route_kernel(xn_ref, wqt_ref, sk_ref, pos_ref, e1_ref, c1_ref, e2_ref, r2_ref, qt_ref):
    qt_ref[...] = lax.dot_general(wqt_ref[...], xn_ref[...], _NT, preferred_element_type=F32)
    n = ROUTE_LANES
    n_tiles = xn_ref.shape[0] // n
    k = PEER_TOPK
    key_iota = lax.broadcasted_iota(jnp.int32, (PEER_NKEYS, n), 0).astype(F32)
    row16 = lax.broadcasted_iota(jnp.int32, (k, n), 0)
    neg = jnp.float32(-jnp.inf)
    invalid = float(_CAND_INVALID)

    def tiles(t, carry):
        first = t * ROUTE_TILES_PER_ITER
        taken = [tile(first + u, exact=False) for u in range(ROUTE_TILES_PER_ITER)]

        @pl.when(jnp.max(functools.reduce(jnp.maximum, taken)) > float(k))
        def _():
            for u in range(ROUTE_TILES_PER_ITER):
                tile(first + u, exact=True)
        return carry

    def tile(t, exact):
        h = t // n_tiles
        lanes = pl.ds(pl.multiple_of((t % n_tiles) * n, n), n)
        base = pl.multiple_of(h * PEER_QDIM, PEER_QDIM)
        q1 = qt_ref[pl.ds(base, PEER_HALF), lanes].astype(BF16)
        q2 = qt_ref[pl.ds(base + PEER_HALF, PEER_HALF), lanes].astype(BF16)
        s1 = jnp.dot(sk_ref[0, h], q1, preferred_element_type=F32)
        s2 = jnp.dot(sk_ref[1, h], q2, preferred_element_type=F32)
        if exact:
            v1, rank1, taken1 = _top16(s1, key_iota, True)
            v2, rank2, taken2 = _top16(s2, key_iota, True)
        else:
            rows1, taken1 = _sorted_top16(s1)
            rows2, taken2 = _sorted_top16(s2)
            v1 = jnp.concatenate([r[0:1] for r in rows1], axis=0)
            v2 = jnp.concatenate([r[0:1] for r in rows2], axis=0)
            ranks = []
            for piece in _key_groups(s2):
                rk = jnp.full(piece.shape, float(k), F32)
                for b in range(k):
                    rk = jnp.where(piece == rows2[b], float(b), rk)
                ranks.append(rk)
            rank2 = jnp.concatenate(ranks, axis=0)
        pos = pos_ref[...]
        cand = jnp.concatenate([v1[0:1] + v2, v1 + v2[0:1], v1[1:2] + v2[0:8], v1[0:8] + v2[1:2],
                                v1[2:3] + v2[0:8], v1[3:4] + v2[0:8], v1[4:5] + v2[0:8]], axis=0)
        cand = jnp.where(pos < invalid, cand, neg)
        best = cand[0:1, :]
        rem = cand
        for _ in range(k):
            m = jnp.max(rem, axis=0, keepdims=True)
            sel = rem == m
            if exact:
                idx = jnp.min(jnp.where(sel, pos, invalid), axis=0, keepdims=True)
                sel = pos == idx
            rem = jnp.where(sel, neg, rem)
        chosen = rem != cand
        z = jnp.sum(jnp.where(chosen, jnp.exp(cand - best), 0.0), axis=0, keepdims=True)
        ch = chosen.astype(F32)
        rowsum = lambda lo, hi: jnp.sum(ch[lo:hi], axis=0, keepdims=True)
        cnt = ch[16:32] + jnp.concatenate([ch[40:48], jnp.zeros((8, n), F32)], axis=0)
        for a, (lo, hi) in enumerate(((0, 16), (32, 40), (48, 56), (56, 64), (64, 72))):
            cnt = cnt + jnp.where(row16 == a, rowsum(lo, hi), 0.0)
        c1 = jnp.zeros((PEER_NKEYS, n), F32)
        for a in range(k):
            is_rank_a = (rank1 == float(a)) if exact else (s1 == v1[a:a + 1])
            c1 = jnp.where(is_rank_a, cnt[a:a + 1], c1)
        e1_ref[h, :, lanes] = jnp.exp(s1 - v1[0:1, :]) * (1.0 / z)
        c1_ref[h, :, lanes] = c1
        e2_ref[h, :, lanes] = jnp.exp(s2 - v2[0:1, :]).astype(e2_ref.dtype)
        r2_ref[h, :, lanes] = rank2.astype(r2_ref.dtype)
        taken_pairs = jnp.sum(ch, axis=0, keepdims=True)
        return jnp.maximum(jnp.maximum(taken1, taken2), taken_pairs)

    lax.fori_loop(0, PEER_HEADS * n_tiles // ROUTE_TILES_PER_ITER, tiles, 0)


def _route(xn2, wq_t, sk):
    t = xn2.shape[0]
    tr = min(512, t)
    tab = lambda dt: jax.ShapeDtypeStruct((PEER_HEADS, PEER_NKEYS, t), dt)
    tab_spec = pl.BlockSpec((PEER_HEADS, PEER_NKEYS, tr), lambda i: (0, 0, i))
    return pl.pallas_call(
        _route_kernel,
        grid=(t // tr,),
        in_specs=[pl.BlockSpec((tr, D_MODEL), lambda i: (i, 0)),
                  pl.BlockSpec((PEER_HEADS * PEER_QDIM, D_MODEL), lambda i: (0, 0)),
                  pl.BlockSpec((2, PEER_HEADS, PEER_NKEYS, PEER_HALF), lambda i: (0, 0, 0, 0)),
                  pl.BlockSpec((_CAND_ROWS, ROUTE_LANES), lambda i: (0, 0))],
        out_specs=[tab_spec] * 4,
        out_shape=[tab(F32), tab(F32), tab(BF16), tab(BF16)],
        scratch_shapes=[pltpu.VMEM((PEER_HEADS * PEER_QDIM, tr), F32)],
        compiler_params=_params("arbitrary"),
        name="route",
    )(xn2, wq_t, sk, _staircase_positions())


PEER_CHUNK = 512
PEER_CHUNKS_PER_STEP = 2
PEER_TOKENS = 512
PEER_WEIGHT_LANES = 256
BF16_ROWS = 16
PEER_VMEM_LIMIT_BYTES = 63 * 1024 * 1024


def _peer_kernel(xn_ref, e1_ref, c1_ref, e2_ref, r2_ref, u_ref, vt_ref, h1_ref, mod_ref, fg_ref,
                 o_ref, acc_ref, xs_ref):
    j = pl.program_id(1)
    ec = PEER_CHUNK
    n = xn_ref.shape[0]
    lanes = PEER_WEIGHT_LANES
    rows_per_chunk = ec // PEER_NKEYS
    rows_per_step = PEER_CHUNKS_PER_STEP * rows_per_chunk

    @pl.when(j == 0)
    def _():
        acc_ref[...] = jnp.zeros_like(acc_ref)
        xs_ref[...] = xn_ref[...]

    zero = jnp.zeros((), BF16)
    keys = pl.ds(pl.multiple_of(j * rows_per_step, rows_per_step), rows_per_step)
    chains = [(kk, tc) for kk in range(PEER_CHUNKS_PER_STEP) for tc in range(n // lanes)]

    def routing_weights(kk, tc):
        cols = slice(tc * lanes, (tc + 1) * lanes)
        row = (BF16_ROWS, lanes)
        reps = PEER_NKEYS // BF16_ROWS
        blocks = []
        for r in range(rows_per_chunk):
            i1 = kk * rows_per_chunk + r
            w = None
            for h in range(PEER_HEADS):
                e1 = jnp.broadcast_to(e1_ref[h, keys, cols][i1:i1 + 1], row).astype(BF16)
                c1 = jnp.broadcast_to(c1_ref[h, keys, cols][i1:i1 + 1], row).astype(BF16)
                e1 = jnp.concatenate([e1] * reps, axis=0)
                c1 = jnp.concatenate([c1] * reps, axis=0)
                t = e1 * jnp.where(r2_ref[h, :, cols] < c1, e2_ref[h, :, cols], zero)
                w = t if w is None else w + t
            blocks.append(w)
        return jnp.concatenate(blocks, axis=0)

    acts = [lax.dot_general(u_ref[kk * ec:(kk + 1) * ec, :], xs_ref[tc * lanes:(tc + 1) * lanes, :], _NT,
                            preferred_element_type=F32) for kk, tc in chains]
    for (kk, tc), act in zip(chains, acts):
        p = routing_weights(kk, tc) * _gelu(act).astype(BF16)
        acc_ref[:, tc * lanes:(tc + 1) * lanes] += jnp.dot(vt_ref[:, kk * ec:(kk + 1) * ec], p,
                                                           preferred_element_type=F32)

    @pl.when(j == pl.num_programs(1) - 1)
    def _():
        h2 = h1_ref[...] + mod_ref[0, 5:6, :] * acc_ref[...].T
        o_ref[...] = h2 * lax.rsqrt(jnp.mean(h2 * h2, axis=-1, keepdims=True) + EPS) * fg_ref[...]


def _peer(xn2, tabs, u, v_t, h1, mod3, final_g, seq):
    t = xn2.shape[0]
    tb = min(PEER_TOKENS, t)
    ecs = PEER_CHUNKS_PER_STEP * PEER_CHUNK
    tab_spec = pl.BlockSpec((PEER_HEADS, PEER_NKEYS, tb), lambda i, j: (0, 0, i))
    return pl.pallas_call(
        _peer_kernel,
        grid=(t // tb, PEER_EXPERTS // ecs),
        in_specs=[pl.BlockSpec((tb, D_MODEL), lambda i, j: (i, 0)),
                  tab_spec, tab_spec, tab_spec, tab_spec,
                  pl.BlockSpec((ecs, D_MODEL), lambda i, j: (j, 0)),
                  pl.BlockSpec((D_MODEL, ecs), lambda i, j: (0, j)),
                  pl.BlockSpec((tb, D_MODEL), lambda i, j: (i, 0)),
                  pl.BlockSpec((1, N_MOD, D_MODEL), lambda i, j: ((i * tb) // seq, 0, 0)),
                  pl.BlockSpec((1, D_MODEL), lambda i, j: (0, 0))],
        out_specs=pl.BlockSpec((tb, D_MODEL), lambda i, j: (i, 0)),
        out_shape=jax.ShapeDtypeStruct((t, D_MODEL), F32),
        scratch_shapes=[pltpu.VMEM((D_MODEL, tb), F32), pltpu.VMEM((tb, D_MODEL), BF16)],
        compiler_params=pltpu.CompilerParams(dimension_semantics=("arbitrary", "arbitrary"),
                                             vmem_limit_bytes=PEER_VMEM_LIMIT_BYTES),
        name="peer",
    )(xn2, *tabs, u, v_t, h1, mod3, final_g)


def kernel(x, c, ctx, c_ctx, norm1_g, norm2_g, w_mod, b_mod, w_in, w_gate_up, b_gate, gla_norm_g,
           cmlp_ln_g, cmlp_ln_b, w_spatial, b_spatial, w_out, peer_wq, peer_sub_keys, peer_u, peer_v,
           final_norm_g):
    batch, seq, d = x.shape
    ctx_len = ctx.shape[1]
    depth = w_mod.shape[0]
    assert d == D_MODEL and depth == 1 and batch + 1 <= MOD_ROWS
    assert seq % MAX_TOKEN_TILE == 0 and ctx_len % GLA_CHUNK == 0
    i = 0

    cc = jnp.zeros((MOD_ROWS, d), F32).at[:batch].set(c).at[batch].set(c_ctx)
    mod3 = _modulation(cc, w_mod[i], b_mod[i]).reshape(MOD_ROWS, N_MOD, d)

    qk_end = 2 * GLA_KEY_WIDTH + 2 * GLA_WIDTH
    lr_end = qk_end + 2 * GLA_GATE_RANK
    w = w_in[i].astype(BF16)
    w_main = jnp.concatenate([w[:, :qk_end], w[:, lr_end:]], axis=1)
    w_lr = jnp.pad(w[:, qk_end:lr_end], ((0, 0), (0, LR_PAD - 2 * GLA_GATE_RANK)))
    n1 = norm1_g[i].reshape(1, d)
    x2 = x.reshape(batch * seq, d)
    p_lat, lr_lat = _inproj(x2, mod3, n1, w_main, w_lr, lambda t, tm: (t * tm) // seq)
    p_ctx, lr_ctx = _inproj(ctx.reshape(batch * ctx_len, d), mod3, n1, w_main, w_lr, lambda t, tm: batch)

    w_up = jnp.zeros((2, LR_PAD, GLA_KEY_WIDTH), F32)
    for dd in range(2):
        w_up = w_up.at[dd, dd * GLA_GATE_RANK:(dd + 1) * GLA_GATE_RANK].set(w_gate_up[i, dd])
    w_up = w_up.astype(BF16)
    bg = b_gate[i].reshape(2, 1, GLA_KEY_WIDTH)
    s_zero = jnp.zeros((batch, 2, GLA_HEADS, GLA_DV, GLA_DK), F32)
    _, _, s_ctx = _gla(p_ctx, lr_ctx, w_up, bg, s_zero, batch, ctx_len)
    o_f, o_b, _ = _gla(p_lat, lr_lat, w_up, bg, s_ctx, batch, seq)

    cm = _cmlp(p_lat, cmlp_ln_g[i].reshape(1, -1), cmlp_ln_b[i].reshape(1, -1),
               w_spatial[i].astype(BF16), b_spatial[i].T)

    h1, xn2 = _outproj(o_f, o_b, p_lat, cm, x2, mod3, gla_norm_g[i].reshape(1, -1),
                       norm2_g[i].reshape(1, d), w_out[i].astype(BF16), seq)

    tabs = _route(xn2, peer_wq[i].T.astype(BF16), peer_sub_keys[i].astype(BF16))
    out = _peer(xn2, tabs, peer_u[i].astype(BF16), peer_v[i].T.astype(BF16), h1, mod3,
                final_norm_g.reshape(1, d), seq)
    return out.reshape(batch, seq, d)
```

```python
import functools

import jax
import jax.numpy as jnp
from jax import lax
from jax.experimental import pallas as pl
from jax.experimental.pallas import tpu as pltpu

F32 = jnp.float32
BF16 = jnp.bfloat16

D_MODEL = 2048
GLA_HEADS = 4
GLA_DK = 128
GLA_DV = 256
GLA_KEY_WIDTH = GLA_HEADS * GLA_DK
GLA_WIDTH = GLA_HEADS * GLA_DV
GLA_GATE_RANK = 16
GLA_GATE_TAU = 16.0
GLA_CHUNK = 64
CMLP_GROUPS = 8
CMLP_GROUP_DIM = 128
CMLP_WIDTH = CMLP_GROUPS * CMLP_GROUP_DIM
CMLP_CHUNK = 128
PEER_HEADS = 8
PEER_NKEYS = 128
PEER_EXPERTS = PEER_NKEYS * PEER_NKEYS
PEER_QDIM = 256
PEER_HALF = PEER_QDIM // 2
PEER_TOPK = 16
N_MOD = 6
EPS = 1e-6

MAIN_WIDTH = 2 * GLA_KEY_WIDTH + 2 * GLA_WIDTH + 2 * CMLP_WIDTH
LR_PAD = 128
MOD_ROWS = 8
MAX_TOKEN_TILE = 1024

VMEM_LIMIT_BYTES = 56 * 1024 * 1024

_NT = (((1,), (1,)), ((), ()))
_TN = (((0,), (0,)), ((), ()))


def _params(*sem):
    return pltpu.CompilerParams(dimension_semantics=sem, vmem_limit_bytes=VMEM_LIMIT_BYTES)


def _gelu(x):
    return 0.5 * x * (1.0 + lax.erf(x * (2.0 ** -0.5)))


def _silu(x):
    return x * jax.nn.sigmoid(x)


def _mod_kernel(cc_ref, w_ref, b_ref, o_ref):
    s = _silu(cc_ref[...])
    o_ref[...] = jnp.dot(s, w_ref[...], precision=lax.Precision.HIGHEST,
                         preferred_element_type=F32) + b_ref[...]


def _modulation(cc, w_mod, b_mod):
    n = w_mod.shape[1]
    tn = 1024
    return pl.pallas_call(
        _mod_kernel,
        grid=(n // tn,),
        in_specs=[pl.BlockSpec((MOD_ROWS, D_MODEL), lambda j: (0, 0)),
                  pl.BlockSpec((D_MODEL, tn), lambda j: (0, j)),
                  pl.BlockSpec((1, tn), lambda j: (0, j))],
        out_specs=pl.BlockSpec((MOD_ROWS, tn), lambda j: (0, j)),
        out_shape=jax.ShapeDtypeStruct((MOD_ROWS, n), F32),
        compiler_params=_params("arbitrary"),
        name="mod",
    )(cc, w_mod, b_mod.reshape(1, n))


def _inproj_kernel(x_ref, mod_ref, g_ref, w_ref, wlr_ref, p_ref, lr_ref, xn_ref):
    @pl.when(pl.program_id(1) == 0)
    def _():
        x = x_ref[...]
        y = x * lax.rsqrt(jnp.mean(x * x, axis=-1, keepdims=True) + EPS) * g_ref[...]
        xn = (y * (1.0 + mod_ref[0, 1:2, :]) + mod_ref[0, 0:1, :]).astype(BF16)
        xn_ref[...] = xn
        lr_ref[...] = jnp.dot(xn, wlr_ref[...], preferred_element_type=F32).astype(lr_ref.dtype)

    p_ref[...] = jnp.dot(xn_ref[...], w_ref[...], preferred_element_type=F32).astype(p_ref.dtype)


def _inproj(x2, mod3, norm_g, w_main, w_lr, mod_row_of_tile):
    t = x2.shape[0]
    tm = min(MAX_TOKEN_TILE, t)
    tn = 1024
    return pl.pallas_call(
        _inproj_kernel,
        grid=(t // tm, MAIN_WIDTH // tn),
        in_specs=[pl.BlockSpec((tm, D_MODEL), lambda i, j: (i, 0)),
                  pl.BlockSpec((1, N_MOD, D_MODEL), lambda i, j: (mod_row_of_tile(i, tm), 0, 0)),
                  pl.BlockSpec((1, D_MODEL), lambda i, j: (0, 0)),
                  pl.BlockSpec((D_MODEL, tn), lambda i, j: (0, j)),
                  pl.BlockSpec((D_MODEL, LR_PAD), lambda i, j: (0, 0))],
        out_specs=[pl.BlockSpec((tm, tn), lambda i, j: (i, j)),
                   pl.BlockSpec((tm, LR_PAD), lambda i, j: (i, 0))],
        out_shape=[jax.ShapeDtypeStruct((t, MAIN_WIDTH), BF16),
                   jax.ShapeDtypeStruct((t, LR_PAD), BF16)],
        scratch_shapes=[pltpu.VMEM((tm, D_MODEL), BF16)],
        compiler_params=_params("arbitrary", "arbitrary"),
        name="inproj",
    )(x2, mod3, norm_g, w_main, w_lr)


def _gla_kernel(qf_ref, kf_ref, vf_ref, lrf_ref, qb_ref, kb_ref, vb_ref, lrb_ref,
                wup_ref, bg_ref, s0_ref, of_ref, ob_ref, st_ref, *, n_chunks):
    @pl.when(pl.program_id(1) == 0)
    def _():
        st_ref[...] = s0_ref[...]

    c = GLA_CHUNK
    row = lax.broadcasted_iota(jnp.int32, (c, c), 0)
    col = lax.broadcasted_iota(jnp.int32, (c, c), 1)
    keep = (col <= row, col >= row)
    q_scale = GLA_DK ** -0.5
    dirs = ((qf_ref, kf_ref, vf_ref, lrf_ref, of_ref), (qb_ref, kb_ref, vb_ref, lrb_ref, ob_ref))

    for d, (q_ref, k_ref, v_ref, lr_ref, o_ref) in enumerate(dirs):
        order = range(n_chunks) if d == 0 else range(n_chunks - 1, -1, -1)
        chunk = lambda ci: slice(ci * c, (ci + 1) * c)
        z = jnp.dot(lr_ref[...], wup_ref[d], preferred_element_type=F32) + bg_ref[d]
        la = (jnp.minimum(z, 0.0) - jnp.log1p(jnp.exp(-jnp.abs(z)))) * (1.0 / GLA_GATE_TAU)
        tri = keep[d].astype(BF16)
        la_hi = la.astype(BF16)
        la_lo = (la - la_hi.astype(F32)).astype(BF16)
        cums = [jnp.dot(tri, la_hi[chunk(ci)], preferred_element_type=F32)
                + jnp.dot(tri, la_lo[chunk(ci)], preferred_element_type=F32)
                for ci in range(n_chunks)]
        tots = [cm[c - 1:c, :] if d == 0 else cm[0:1, :] for cm in cums]
        cum = jnp.concatenate(cums, axis=0)
        tot = jnp.concatenate([jnp.broadcast_to(t, (c, t.shape[1])) for t in tots], axis=0)
        kf = k_ref[...].astype(F32)
        qd = (q_ref[...].astype(F32) * q_scale * jnp.exp(cum)).astype(BF16)
        kd = (kf * jnp.exp(-cum)).astype(BF16)
        k2 = (kf * jnp.exp(tot - cum)).astype(BF16)
        etots = [jnp.exp(t) for t in tots]
        for h in range(GLA_HEADS):
            ks = slice(h * GLA_DK, (h + 1) * GLA_DK)
            vs = slice(h * GLA_DV, (h + 1) * GLA_DV)
            intra, incr = [], []
            for ci in range(n_chunks):
                rows = chunk(ci)
                vh = v_ref[rows, vs]
                sc = lax.dot_general(qd[rows, ks], kd[rows, ks], _NT, preferred_element_type=F32)
                sc = jnp.where(keep[d], sc, 0.0).astype(BF16)
                intra.append(jnp.dot(sc, vh, preferred_element_type=F32))
                incr.append(lax.dot_general(vh, k2[rows, ks], _TN, preferred_element_type=F32))
            st = st_ref[0, d, h]
            for ci in order:
                rows = chunk(ci)
                o_ref[rows, vs] = intra[ci] + lax.dot_general(qd[rows, ks], st.astype(BF16), _NT,
                                                              preferred_element_type=F32)
                st = st * etots[ci][:, ks] + incr[ci]
            st_ref[0, d, h] = st


def _gla(p, lr, w_up, b_gate, s0, batch, seq):
    tb = min(512, seq)
    nb = seq // tb
    qw, vw = GLA_KEY_WIDTH, GLA_WIDTH
    fwd = lambda col: (lambda b, s: (b * nb + s, col))
    bwd = lambda col: (lambda b, s: (b * nb + nb - 1 - s, col))
    state_spec = pl.BlockSpec((1, 2, GLA_HEADS, GLA_DV, GLA_DK), lambda b, s: (b, 0, 0, 0, 0))
    in_specs = []
    for m in (fwd, bwd):
        in_specs += [pl.BlockSpec((tb, qw), m(0)), pl.BlockSpec((tb, qw), m(1)),
                     pl.BlockSpec((tb, vw), m(1)), pl.BlockSpec((tb, LR_PAD), m(0))]
    in_specs += [pl.BlockSpec((2, LR_PAD, qw), lambda b, s: (0, 0, 0)),
                 pl.BlockSpec((2, 1, qw), lambda b, s: (0, 0, 0)),
                 state_spec]
    return pl.pallas_call(
        functools.partial(_gla_kernel, n_chunks=tb // GLA_CHUNK),
        grid=(batch, nb),
        in_specs=in_specs,
        out_specs=[pl.BlockSpec((tb, vw), fwd(0)), pl.BlockSpec((tb, vw), bwd(0)), state_spec],
        out_shape=[jax.ShapeDtypeStruct((batch * seq, vw), F32),
                   jax.ShapeDtypeStruct((batch * seq, vw), F32),
                   jax.ShapeDtypeStruct((batch, 2, GLA_HEADS, GLA_DV, GLA_DK), F32)],
        compiler_params=_params("arbitrary", "arbitrary"),
        name="gla",
    )(p, p, p, lr, p, p, p, lr, w_up, b_gate, s0)


def _cmlp_kernel(u_ref, v_ref, lng_ref, lnb_ref, ws_ref, bs_ref, o_ref, *, n_chunks):
    gv = _gelu(v_ref[...].astype(F32))
    mu = jnp.mean(gv, axis=-1, keepdims=True)
    dv = gv - mu
    var = jnp.mean(dv * dv, axis=-1, keepdims=True)
    vn = (dv * lax.rsqrt(var + EPS) * lng_ref[...] + lnb_ref[...]).astype(BF16)
    for ch in range(n_chunks):
        rows = slice(ch * CMLP_CHUNK, (ch + 1) * CMLP_CHUNK)
        for g in range(CMLP_GROUPS):
            cols = slice(g * CMLP_GROUP_DIM, (g + 1) * CMLP_GROUP_DIM)
            s = jnp.dot(ws_ref[g], vn[rows, cols], preferred_element_type=F32) + bs_ref[:, g:g + 1]
            o_ref[rows, cols] = (_gelu(u_ref[rows, cols].astype(F32)) * s).astype(o_ref.dtype)


def _cmlp(p, ln_g, ln_b, w_s, b_st):
    t = p.shape[0]
    tm = min(512, t)
    cu_blk = (2 * GLA_KEY_WIDTH + 2 * GLA_WIDTH) // CMLP_WIDTH
    return pl.pallas_call(
        functools.partial(_cmlp_kernel, n_chunks=tm // CMLP_CHUNK),
        grid=(t // tm,),
        in_specs=[pl.BlockSpec((tm, CMLP_WIDTH), lambda i: (i, cu_blk)),
                  pl.BlockSpec((tm, CMLP_WIDTH), lambda i: (i, cu_blk + 1)),
                  pl.BlockSpec((1, CMLP_WIDTH), lambda i: (0, 0)),
                  pl.BlockSpec((1, CMLP_WIDTH), lambda i: (0, 0)),
                  pl.BlockSpec((CMLP_GROUPS, CMLP_CHUNK, CMLP_CHUNK), lambda i: (0, 0, 0)),
                  pl.BlockSpec((CMLP_CHUNK, CMLP_GROUPS), lambda i: (0, 0))],
        out_specs=pl.BlockSpec((tm, CMLP_WIDTH), lambda i: (i, 0)),
        out_shape=jax.ShapeDtypeStruct((t, CMLP_WIDTH), BF16),
        compiler_params=_params("arbitrary"),
        name="cmlp",
    )(p, p, ln_g, ln_b, w_s, b_st)


def _outproj_kernel(of_ref, ob_ref, g_ref, cm_ref, x_ref, mod_ref, gng_ref, n2g_ref, w_ref,
                    h1_ref, xn2_ref):
    o = of_ref[...] + ob_ref[...]
    heads = []
    for h in range(GLA_HEADS):
        oh = o[:, h * GLA_DV:(h + 1) * GLA_DV]
        heads.append(oh * lax.rsqrt(jnp.mean(oh * oh, axis=-1, keepdims=True) + EPS))
    on = jnp.concatenate(heads, axis=-1) * gng_ref[...]
    y = (on * _silu(g_ref[...].astype(F32))).astype(BF16)
    mix = (jnp.dot(y, w_ref[0:GLA_WIDTH, :], preferred_element_type=F32)
           + jnp.dot(cm_ref[...], w_ref[GLA_WIDTH:GLA_WIDTH + CMLP_WIDTH, :], preferred_element_type=F32))
    h1 = x_ref[...] + mod_ref[0, 2:3, :] * mix
    h1_ref[...] = h1
    hn = h1 * lax.rsqrt(jnp.mean(h1 * h1, axis=-1, keepdims=True) + EPS) * n2g_ref[...]
    xn2_ref[...] = (hn * (1.0 + mod_ref[0, 4:5, :]) + mod_ref[0, 3:4, :]).astype(xn2_ref.dtype)


def _outproj(o_f, o_b, p, cm, x2, mod3, gla_norm_g, norm2_g, w_out, seq):
    t = x2.shape[0]
    tm = min(512, t)
    g_blk = (2 * GLA_KEY_WIDTH + GLA_WIDTH) // GLA_WIDTH
    return pl.pallas_call(
        _outproj_kernel,
        grid=(t // tm,),
        in_specs=[pl.BlockSpec((tm, GLA_WIDTH), lambda i: (i, 0)),
                  pl.BlockSpec((tm, GLA_WIDTH), lambda i: (i, 0)),
                  pl.BlockSpec((tm, GLA_WIDTH), lambda i: (i, g_blk)),
                  pl.BlockSpec((tm, CMLP_WIDTH), lambda i: (i, 0)),
                  pl.BlockSpec((tm, D_MODEL), lambda i: (i, 0)),
                  pl.BlockSpec((1, N_MOD, D_MODEL), lambda i: ((i * tm) // seq, 0, 0)),
                  pl.BlockSpec((1, GLA_WIDTH), lambda i: (0, 0)),
                  pl.BlockSpec((1, D_MODEL), lambda i: (0, 0)),
                  pl.BlockSpec((GLA_WIDTH + CMLP_WIDTH, D_MODEL), lambda i: (0, 0))],
        out_specs=[pl.BlockSpec((tm, D_MODEL), lambda i: (i, 0)),
                   pl.BlockSpec((tm, D_MODEL), lambda i: (i, 0))],
        out_shape=[jax.ShapeDtypeStruct((t, D_MODEL), F32),
                   jax.ShapeDtypeStruct((t, D_MODEL), BF16)],
        compiler_params=_params("arbitrary"),
        name="outproj",
    )(o_f, o_b, p, cm, x2, mod3, gla_norm_g, norm2_g, w_out)


def _top16(s, key_iota, exact):
    neg = jnp.float32(-jnp.inf)
    vals = []
    rank = jnp.full(s.shape, float(PEER_TOPK), F32)
    for r in range(PEER_TOPK):
        m = jnp.max(s, axis=0, keepdims=True)
        sel = s == m
        if exact:
            idx = jnp.min(jnp.where(sel, key_iota, float(s.shape[0])), axis=0, keepdims=True)
            sel = key_iota == idx
        rank = jnp.where(sel, float(r), rank)
        s = jnp.where(sel, neg, s)
        vals.append(m)
    taken = jnp.sum(jnp.where(s == neg, 1.0, 0.0), axis=0, keepdims=True)
    return jnp.concatenate(vals, axis=0), rank, taken


SUBLANES = 8


def _key_groups(s):
    return [s[q * SUBLANES:(q + 1) * SUBLANES] for q in range(s.shape[0] // SUBLANES)]


def _oddeven_mergesort_pairs(n):
    pairs = []

    def merge(lo, hi, r):
        step = r * 2
        if step < hi - lo:
            merge(lo, hi, step)
            merge(lo + r, hi, step)
            pairs.extend((i, i + r) for i in range(lo + r, hi - r, step))
        else:
            pairs.append((lo, lo + r))

    def sort(lo, hi):
        if hi - lo >= 1:
            mid = lo + (hi - lo) // 2
            sort(lo, mid)
            sort(mid + 1, hi)
            merge(lo, hi, 1)

    sort(0, n - 1)
    return pairs


def _sorted_top16(s):
    k = PEER_TOPK
    groups = _key_groups(s)
    assert len(groups) <= k
    rows = groups + [None] * (k - len(groups))

    def vmax(a, b):
        return b if a is None else a if b is None else jnp.maximum(a, b)

    def exchange(i, j):
        if rows[j] is None:
            return
        if rows[i] is None:
            rows[i], rows[j] = rows[j], None
        else:
            rows[i], rows[j] = jnp.maximum(rows[i], rows[j]), jnp.minimum(rows[i], rows[j])

    for i, j in _oddeven_mergesort_pairs(k):
        exchange(i, j)
    shift = SUBLANES // 2
    while shift:
        other = [None if r is None else pltpu.roll(r, shift, 0) for r in rows]
        rows = [vmax(rows[i], other[k - 1 - i]) for i in range(k)]
        d = k // 2
        while d:
            for i in range(k):
                if not i & d:
                    exchange(i, i + d)
            d //= 2
        shift //= 2
    reach = sum(jnp.where(g >= rows[k - 1], 1.0, 0.0) for g in groups)
    repeats = sum(jnp.where(rows[b] == rows[b + 1], 1.0, 0.0) for b in range(k - 1))
    taken = jnp.sum(reach, axis=0, keepdims=True) + float(k) * repeats[0:1]
    return rows, taken


_CAND_INVALID = 1 << 20
_CAND_ROWS = 72


def _staircase_positions():
    assert PEER_TOPK == 16
    k, inv = PEER_TOPK, _CAND_INVALID
    pos = list(range(k))
    pos += [inv] + [a * k for a in range(1, k)]
    pos += [inv] + [k + b for b in range(1, 8)]
    pos += [inv, inv] + [a * k + 1 for a in range(2, 8)]
    pos += [2 * k + b if b in (2, 3, 4) else inv for b in range(8)]
    pos += [3 * k + b if b in (2, 3) else inv for b in range(8)]
    pos += [4 * k + b if b == 2 else inv for b in range(8)]
    assert len(pos) == _CAND_ROWS and sum(p != inv for p in pos) == 50
    return jnp.broadcast_to(jnp.array(pos, F32)[:, None], (_CAND_ROWS, ROUTE_LANES))


ROUTE_LANES = 128
ROUTE_TILES_PER_ITER = 4


def _route_kernel(xn_ref, wqt_ref, sk_ref, pos_ref, e1_ref, c1_ref, e2_ref, r2_ref, qt_ref):
    qt_ref[...] = lax.dot_general(wqt_ref[...], xn_ref[...], _NT, preferred_element_type=F32)
    n = ROUTE_LANES
    n_tiles = xn_ref.shape[0] // n
    k = PEER_TOPK
    key_iota = lax.broadcasted_iota(jnp.int32, (PEER_NKEYS, n), 0).astype(F32)
    row16 = lax.broadcasted_iota(jnp.int32, (k, n), 0)
    neg = jnp.float32(-jnp.inf)
    invalid = float(_CAND_INVALID)

    def tiles(t, carry):
        first = t * ROUTE_TILES_PER_ITER
        taken = [tile(first + u, exact=False) for u in range(ROUTE_TILES_PER_ITER)]

        @pl.when(jnp.max(functools.reduce(jnp.maximum, taken)) > float(k))
        def _():
            for u in range(ROUTE_TILES_PER_ITER):
                tile(first + u, exact=True)
        return carry

    def tile(t, exact):
        h = t // n_tiles
        lanes = pl.ds(pl.multiple_of((t % n_tiles) * n, n), n)
        base = pl.multiple_of(h * PEER_QDIM, PEER_QDIM)
        q1 = qt_ref[pl.ds(base, PEER_HALF), lanes].astype(BF16)
        q2 = qt_ref[pl.ds(base + PEER_HALF, PEER_HALF), lanes].astype(BF16)
        s1 = jnp.dot(sk_ref[0, h], q1, preferred_element_type=F32)
        s2 = jnp.dot(sk_ref[1, h], q2, preferred_element_type=F32)
        if exact:
            v1, rank1, taken1 = _top16(s1, key_iota, True)
            v2, rank2, taken2 = _top16(s2, key_iota, True)
        else:
            rows1, taken1 = _sorted_top16(s1)
            rows2, taken2 = _sorted_top16(s2)
            v1 = jnp.concatenate([r[0:1] for r in rows1], axis=0)
            v2 = jnp.concatenate([r[0:1] for r in rows2], axis=0)
            ranks = []
            for piece in _key_groups(s2):
                rk = jnp.full(piece.shape, float(k), F32)
                for b in range(k):
                    rk = jnp.where(piece == rows2[b], float(b), rk)
                ranks.append(rk)
            rank2 = jnp.concatenate(ranks, axis=0)
        pos = pos_ref[...]
        cand = jnp.concatenate([v1[0:1] + v2, v1 + v2[0:1], v1[1:2] + v2[0:8], v1[0:8] + v2[1:2],
                                v1[2:3] + v2[0:8], v1[3:4] + v2[0:8], v1[4:5] + v2[0:8]], axis=0)
        cand = jnp.where(pos < invalid, cand, neg)
        best = cand[0:1, :]
        if exact:
            rem = cand
            for _ in range(k):
                m = jnp.max(rem, axis=0, keepdims=True)
                idx = jnp.min(jnp.where(rem == m, pos, invalid), axis=0, keepdims=True)
                rem = jnp.where(pos == idx, neg, rem)
            chosen = rem != cand
        else:
            best_pairs, taken_pairs = _sorted_top16(cand)
            chosen = jnp.concatenate([g >= best_pairs[k - 1] for g in _key_groups(cand)], axis=0)
        z = jnp.sum(jnp.where(chosen, jnp.exp(cand - best), 0.0), axis=0, keepdims=True)
        ch = chosen.astype(F32)
        rowsum = lambda lo, hi: jnp.sum(ch[lo:hi], axis=0, keepdims=True)
        cnt = ch[16:32] + jnp.concatenate([ch[40:48], jnp.zeros((8, n), F32)], axis=0)
        for a, (lo, hi) in enumerate(((0, 16), (32, 40), (48, 56), (56, 64), (64, 72))):
            cnt = cnt + jnp.where(row16 == a, rowsum(lo, hi), 0.0)
        c1 = jnp.zeros((PEER_NKEYS, n), F32)
        for a in range(k):
            is_rank_a = (rank1 == float(a)) if exact else (s1 == v1[a:a + 1])
            c1 = jnp.where(is_rank_a, cnt[a:a + 1], c1)
        e1_ref[h, :, lanes] = jnp.exp(s1 - v1[0:1, :]) * (1.0 / z)
        c1_ref[h, :, lanes] = c1
        e2_ref[h, :, lanes] = jnp.exp(s2 - v2[0:1, :]).astype(e2_ref.dtype)
        r2_ref[h, :, lanes] = rank2.astype(r2_ref.dtype)
        if exact:
            return None
        return jnp.maximum(jnp.maximum(taken1, taken2), taken_pairs)

    lax.fori_loop(0, PEER_HEADS * n_tiles // ROUTE_TILES_PER_ITER, tiles, 0)


def _route(xn2, wq_t, sk):
    t = xn2.shape[0]
    tr = min(512, t)
    tab = lambda dt: jax.ShapeDtypeStruct((PEER_HEADS, PEER_NKEYS, t), dt)
    tab_spec = pl.BlockSpec((PEER_HEADS, PEER_NKEYS, tr), lambda i: (0, 0, i))
    return pl.pallas_call(
        _route_kernel,
        grid=(t // tr,),
        in_specs=[pl.BlockSpec((tr, D_MODEL), lambda i: (i, 0)),
                  pl.BlockSpec((PEER_HEADS * PEER_QDIM, D_MODEL), lambda i: (0, 0)),
                  pl.BlockSpec((2, PEER_HEADS, PEER_NKEYS, PEER_HALF), lambda i: (0, 0, 0, 0)),
                  pl.BlockSpec((_CAND_ROWS, ROUTE_LANES), lambda i: (0, 0))],
        out_specs=[tab_spec] * 4,
        out_shape=[tab(F32), tab(F32), tab(BF16), tab(BF16)],
        scratch_shapes=[pltpu.VMEM((PEER_HEADS * PEER_QDIM, tr), F32)],
        compiler_params=_params("arbitrary"),
        name="route",
    )(xn2, wq_t, sk, _staircase_positions())


PEER_CHUNK = 512
PEER_CHUNKS_PER_STEP = 2
PEER_TOKENS = 512
PEER_WEIGHT_LANES = 256
BF16_ROWS = 16
PEER_VMEM_LIMIT_BYTES = 63 * 1024 * 1024


def _peer_kernel(xn_ref, e1_ref, c1_ref, e2_ref, r2_ref, u_ref, vt_ref, h1_ref, mod_ref, fg_ref,
                 o_ref, acc_ref, xs_ref):
    j = pl.program_id(1)
    ec = PEER_CHUNK
    n = xn_ref.shape[0]
    lanes = PEER_WEIGHT_LANES
    rows_per_chunk = ec // PEER_NKEYS
    rows_per_step = PEER_CHUNKS_PER_STEP * rows_per_chunk

    @pl.when(j == 0)
    def _():
        acc_ref[...] = jnp.zeros_like(acc_ref)
        xs_ref[...] = xn_ref[...]

    zero = jnp.zeros((), BF16)
    keys = pl.ds(pl.multiple_of(j * rows_per_step, rows_per_step), rows_per_step)
    chains = [(kk, tc) for kk in range(PEER_CHUNKS_PER_STEP) for tc in range(n // lanes)]

    def routing_weights(kk, tc):
        cols = slice(tc * lanes, (tc + 1) * lanes)
        row = (BF16_ROWS, lanes)
        reps = PEER_NKEYS // BF16_ROWS
        blocks = []
        for r in range(rows_per_chunk):
            i1 = kk * rows_per_chunk + r
            w = None
            for h in range(PEER_HEADS):
                e1 = jnp.broadcast_to(e1_ref[h, keys, cols][i1:i1 + 1], row).astype(BF16)
                c1 = jnp.broadcast_to(c1_ref[h, keys, cols][i1:i1 + 1], row).astype(BF16)
                e1 = jnp.concatenate([e1] * reps, axis=0)
                c1 = jnp.concatenate([c1] * reps, axis=0)
                t = e1 * jnp.where(r2_ref[h, :, cols] < c1, e2_ref[h, :, cols], zero)
                w = t if w is None else w + t
            blocks.append(w)
        return jnp.concatenate(blocks, axis=0)

    acts = [lax.dot_general(u_ref[kk * ec:(kk + 1) * ec, :], xs_ref[tc * lanes:(tc + 1) * lanes, :], _NT,
                            preferred_element_type=F32) for kk, tc in chains]
    for (kk, tc), act in zip(chains, acts):
        p = routing_weights(kk, tc) * _gelu(act).astype(BF16)
        acc_ref[:, tc * lanes:(tc + 1) * lanes] += jnp.dot(vt_ref[:, kk * ec:(kk + 1) * ec], p,
                                                           preferred_element_type=F32)

    @pl.when(j == pl.num_programs(1) - 1)
    def _():
        h2 = h1_ref[...] + mod_ref[0, 5:6, :] * acc_ref[...].T
        o_ref[...] = h2 * lax.rsqrt(jnp.mean(h2 * h2, axis=-1, keepdims=True) + EPS) * fg_ref[...]


def _peer(xn2, tabs, u, v_t, h1, mod3, final_g, seq):
    t = xn2.shape[0]
    tb = min(PEER_TOKENS, t)
    ecs = PEER_CHUNKS_PER_STEP * PEER_CHUNK
    tab_spec = pl.BlockSpec((PEER_HEADS, PEER_NKEYS, tb), lambda i, j: (0, 0, i))
    return pl.pallas_call(
        _peer_kernel,
        grid=(t // tb, PEER_EXPERTS // ecs),
        in_specs=[pl.BlockSpec((tb, D_MODEL), lambda i, j: (i, 0)),
                  tab_spec, tab_spec, tab_spec, tab_spec,
                  pl.BlockSpec((ecs, D_MODEL), lambda i, j: (j, 0)),
                  pl.BlockSpec((D_MODEL, ecs), lambda i, j: (0, j)),
                  pl.BlockSpec((tb, D_MODEL), lambda i, j: (i, 0)),
                  pl.BlockSpec((1, N_MOD, D_MODEL), lambda i, j: ((i * tb) // seq, 0, 0)),
                  pl.BlockSpec((1, D_MODEL), lambda i, j: (0, 0))],
        out_specs=pl.BlockSpec((tb, D_MODEL), lambda i, j: (i, 0)),
        out_shape=jax.ShapeDtypeStruct((t, D_MODEL), F32),
        scratch_shapes=[pltpu.VMEM((D_MODEL, tb), F32), pltpu.VMEM((tb, D_MODEL), BF16)],
        compiler_params=pltpu.CompilerParams(dimension_semantics=("arbitrary", "arbitrary"),
                                             vmem_limit_bytes=PEER_VMEM_LIMIT_BYTES),
        name="peer",
    )(xn2, *tabs, u, v_t, h1, mod3, final_g)


def kernel(x, c, ctx, c_ctx, norm1_g, norm2_g, w_mod, b_mod, w_in, w_gate_up, b_gate, gla_norm_g,
           cmlp_ln_g, cmlp_ln_b, w_spatial, b_spatial, w_out, peer_wq, peer_sub_keys, peer_u, peer_v,
           final_norm_g):
    batch, seq, d = x.shape
    ctx_len = ctx.shape[1]
    depth = w_mod.shape[0]
    assert d == D_MODEL and depth == 1 and batch + 1 <= MOD_ROWS
    assert seq % MAX_TOKEN_TILE == 0 and ctx_len % GLA_CHUNK == 0
    i = 0

    cc = jnp.zeros((MOD_ROWS, d), F32).at[:batch].set(c).at[batch].set(c_ctx)
    mod3 = _modulation(cc, w_mod[i], b_mod[i]).reshape(MOD_ROWS, N_MOD, d)

    qk_end = 2 * GLA_KEY_WIDTH + 2 * GLA_WIDTH
    lr_end = qk_end + 2 * GLA_GATE_RANK
    w = w_in[i].astype(BF16)
    w_main = jnp.concatenate([w[:, :qk_end], w[:, lr_end:]], axis=1)
    w_lr = jnp.pad(w[:, qk_end:lr_end], ((0, 0), (0, LR_PAD - 2 * GLA_GATE_RANK)))
    n1 = norm1_g[i].reshape(1, d)
    x2 = x.reshape(batch * seq, d)
    p_lat, lr_lat = _inproj(x2, mod3, n1, w_main, w_lr, lambda t, tm: (t * tm) // seq)
    p_ctx, lr_ctx = _inproj(ctx.reshape(batch * ctx_len, d), mod3, n1, w_main, w_lr, lambda t, tm: batch)

    w_up = jnp.zeros((2, LR_PAD, GLA_KEY_WIDTH), F32)
    for dd in range(2):
        w_up = w_up.at[dd, dd * GLA_GATE_RANK:(dd + 1) * GLA_GATE_RANK].set(w_gate_up[i, dd])
    w_up = w_up.astype(BF16)
    bg = b_gate[i].reshape(2, 1, GLA_KEY_WIDTH)
    s_zero = jnp.zeros((batch, 2, GLA_HEADS, GLA_DV, GLA_DK), F32)
    _, _, s_ctx = _gla(p_ctx, lr_ctx, w_up, bg, s_zero, batch, ctx_len)
    o_f, o_b, _ = _gla(p_lat, lr_lat, w_up, bg, s_ctx, batch, seq)

    cm = _cmlp(p_lat, cmlp_ln_g[i].reshape(1, -1), cmlp_ln_b[i].reshape(1, -1),
               w_spatial[i].astype(BF16), b_spatial[i].T)

    h1, xn2 = _outproj(o_f, o_b, p_lat, cm, x2, mod3, gla_norm_g[i].reshape(1, -1),
                       norm2_g[i].reshape(1, d), w_out[i].astype(BF16), seq)

    tabs = _route(xn2, peer_wq[i].T.astype(BF16), peer_sub_keys[i].astype(BF16))
    out = _peer(xn2, tabs, peer_u[i].astype(BF16), peer_v[i].T.astype(BF16), h1, mod3,
                final_norm_g.reshape(1, d), seq)
    return out.reshape(batch, seq, d)
```

```python
import functools

import jax
import jax.numpy as jnp
from jax import lax
from jax.experimental import pallas as pl
from jax.experimental.pallas import tpu as pltpu

F32 = jnp.float32
BF16 = jnp.bfloat16

D_MODEL = 2048
GLA_HEADS = 4
GLA_DK = 128
GLA_DV = 256
GLA_KEY_WIDTH = GLA_HEADS * GLA_DK
GLA_WIDTH = GLA_HEADS * GLA_DV
GLA_GATE_RANK = 16
GLA_GATE_TAU = 16.0
GLA_CHUNK = 64
CMLP_GROUPS = 8
CMLP_GROUP_DIM = 128
CMLP_WIDTH = CMLP_GROUPS * CMLP_GROUP_DIM
CMLP_CHUNK = 128
PEER_HEADS = 8
PEER_NKEYS = 128
PEER_EXPERTS = PEER_NKEYS * PEER_NKEYS
PEER_QDIM = 256
PEER_HALF = PEER_QDIM // 2
PEER_TOPK = 16
N_MOD = 6
EPS = 1e-6

MAIN_WIDTH = 2 * GLA_KEY_WIDTH + 2 * GLA_WIDTH + 2 * CMLP_WIDTH
LR_PAD = 128
MOD_ROWS = 8
MAX_TOKEN_TILE = 1024

VMEM_LIMIT_BYTES = 56 * 1024 * 1024

_NT = (((1,), (1,)), ((), ()))
_TN = (((0,), (0,)), ((), ()))


def _params(*sem):
    return pltpu.CompilerParams(dimension_semantics=sem, vmem_limit_bytes=VMEM_LIMIT_BYTES)


def _gelu(x):
    return 0.5 * x * (1.0 + lax.erf(x * (2.0 ** -0.5)))


def _silu(x):
    return x * jax.nn.sigmoid(x)


def _mod_kernel(cc_ref, w_ref, b_ref, o_ref):
    s = _silu(cc_ref[...])
    o_ref[...] = jnp.dot(s, w_ref[...], precision=lax.Precision.HIGHEST,
                         preferred_element_type=F32) + b_ref[...]


def _modulation(cc, w_mod, b_mod):
    n = w_mod.shape[1]
    tn = 1024
    return pl.pallas_call(
        _mod_kernel,
        grid=(n // tn,),
        in_specs=[pl.BlockSpec((MOD_ROWS, D_MODEL), lambda j: (0, 0)),
                  pl.BlockSpec((D_MODEL, tn), lambda j: (0, j)),
                  pl.BlockSpec((1, tn), lambda j: (0, j))],
        out_specs=pl.BlockSpec((MOD_ROWS, tn), lambda j: (0, j)),
        out_shape=jax.ShapeDtypeStruct((MOD_ROWS, n), F32),
        compiler_params=_params("arbitrary"),
        name="mod",
    )(cc, w_mod, b_mod.reshape(1, n))


def _inproj_kernel(x_ref, mod_ref, g_ref, w_ref, wlr_ref, p_ref, lr_ref, xn_ref):
    @pl.when(pl.program_id(1) == 0)
    def _():
        x = x_ref[...]
        y = x * lax.rsqrt(jnp.mean(x * x, axis=-1, keepdims=True) + EPS) * g_ref[...]
        xn = (y * (1.0 + mod_ref[0, 1:2, :]) + mod_ref[0, 0:1, :]).astype(BF16)
        xn_ref[...] = xn
        lr_ref[...] = jnp.dot(xn, wlr_ref[...], preferred_element_type=F32).astype(lr_ref.dtype)

    p_ref[...] = jnp.dot(xn_ref[...], w_ref[...], preferred_element_type=F32).astype(p_ref.dtype)


def _inproj(x2, mod3, norm_g, w_main, w_lr, mod_row_of_tile):
    t = x2.shape[0]
    tm = min(MAX_TOKEN_TILE, t)
    tn = 1024
    return pl.pallas_call(
        _inproj_kernel,
        grid=(t // tm, MAIN_WIDTH // tn),
        in_specs=[pl.BlockSpec((tm, D_MODEL), lambda i, j: (i, 0)),
                  pl.BlockSpec((1, N_MOD, D_MODEL), lambda i, j: (mod_row_of_tile(i, tm), 0, 0)),
                  pl.BlockSpec((1, D_MODEL), lambda i, j: (0, 0)),
                  pl.BlockSpec((D_MODEL, tn), lambda i, j: (0, j)),
                  pl.BlockSpec((D_MODEL, LR_PAD), lambda i, j: (0, 0))],
        out_specs=[pl.BlockSpec((tm, tn), lambda i, j: (i, j)),
                   pl.BlockSpec((tm, LR_PAD), lambda i, j: (i, 0))],
        out_shape=[jax.ShapeDtypeStruct((t, MAIN_WIDTH), BF16),
                   jax.ShapeDtypeStruct((t, LR_PAD), BF16)],
        scratch_shapes=[pltpu.VMEM((tm, D_MODEL), BF16)],
        compiler_params=_params("arbitrary", "arbitrary"),
        name="inproj",
    )(x2, mod3, norm_g, w_main, w_lr)


def _gla_kernel(qf_ref, kf_ref, vf_ref, lrf_ref, qb_ref, kb_ref, vb_ref, lrb_ref,
                wup_ref, bg_ref, s0_ref, of_ref, ob_ref, st_ref, *, n_chunks):
    @pl.when(pl.program_id(1) == 0)
    def _():
        st_ref[...] = s0_ref[...]

    c = GLA_CHUNK
    row = lax.broadcasted_iota(jnp.int32, (c, c), 0)
    col = lax.broadcasted_iota(jnp.int32, (c, c), 1)
    keep = (col <= row, col >= row)
    q_scale = GLA_DK ** -0.5
    dirs = ((qf_ref, kf_ref, vf_ref, lrf_ref, of_ref), (qb_ref, kb_ref, vb_ref, lrb_ref, ob_ref))

    for d, (q_ref, k_ref, v_ref, lr_ref, o_ref) in enumerate(dirs):
        order = range(n_chunks) if d == 0 else range(n_chunks - 1, -1, -1)
        chunk = lambda ci: slice(ci * c, (ci + 1) * c)
        z = jnp.dot(lr_ref[...], wup_ref[d], preferred_element_type=F32) + bg_ref[d]
        la = (jnp.minimum(z, 0.0) - jnp.log1p(jnp.exp(-jnp.abs(z)))) * (1.0 / GLA_GATE_TAU)
        tri = keep[d].astype(BF16)
        la_hi = la.astype(BF16)
        la_lo = (la - la_hi.astype(F32)).astype(BF16)
        cums = [jnp.dot(tri, la_hi[chunk(ci)], preferred_element_type=F32)
                + jnp.dot(tri, la_lo[chunk(ci)], preferred_element_type=F32)
                for ci in range(n_chunks)]
        tots = [cm[c - 1:c, :] if d == 0 else cm[0:1, :] for cm in cums]
        cum = jnp.concatenate(cums, axis=0)
        tot = jnp.concatenate([jnp.broadcast_to(t, (c, t.shape[1])) for t in tots], axis=0)
        kf = k_ref[...].astype(F32)
        qd = (q_ref[...].astype(F32) * q_scale * jnp.exp(cum)).astype(BF16)
        kd = (kf * jnp.exp(-cum)).astype(BF16)
        k2 = (kf * jnp.exp(tot - cum)).astype(BF16)
        etots = [jnp.exp(t) for t in tots]
        for h in range(GLA_HEADS):
            ks = slice(h * GLA_DK, (h + 1) * GLA_DK)
            vs = slice(h * GLA_DV, (h + 1) * GLA_DV)
            intra, incr = [], []
            for ci in range(n_chunks):
                rows = chunk(ci)
                vh = v_ref[rows, vs]
                sc = lax.dot_general(qd[rows, ks], kd[rows, ks], _NT, preferred_element_type=F32)
                sc = jnp.where(keep[d], sc, 0.0).astype(BF16)
                intra.append(jnp.dot(sc, vh, preferred_element_type=F32))
                incr.append(lax.dot_general(vh, k2[rows, ks], _TN, preferred_element_type=F32))
            st = st_ref[0, d, h]
            for ci in order:
                rows = chunk(ci)
                o_ref[rows, vs] = intra[ci] + lax.dot_general(qd[rows, ks], st.astype(BF16), _NT,
                                                              preferred_element_type=F32)
                st = st * etots[ci][:, ks] + incr[ci]
            st_ref[0, d, h] = st


def _gla(p, lr, w_up, b_gate, s0, batch, seq):
    tb = min(512, seq)
    nb = seq // tb
    qw, vw = GLA_KEY_WIDTH, GLA_WIDTH
    fwd = lambda col: (lambda b, s: (b * nb + s, col))
    bwd = lambda col: (lambda b, s: (b * nb + nb - 1 - s, col))
    state_spec = pl.BlockSpec((1, 2, GLA_HEADS, GLA_DV, GLA_DK), lambda b, s: (b, 0, 0, 0, 0))
    in_specs = []
    for m in (fwd, bwd):
        in_specs += [pl.BlockSpec((tb, qw), m(0)), pl.BlockSpec((tb, qw), m(1)),
                     pl.BlockSpec((tb, vw), m(1)), pl.BlockSpec((tb, LR_PAD), m(0))]
    in_specs += [pl.BlockSpec((2, LR_PAD, qw), lambda b, s: (0, 0, 0)),
                 pl.BlockSpec((2, 1, qw), lambda b, s: (0, 0, 0)),
                 state_spec]
    return pl.pallas_call(
        functools.partial(_gla_kernel, n_chunks=tb // GLA_CHUNK),
        grid=(batch, nb),
        in_specs=in_specs,
        out_specs=[pl.BlockSpec((tb, vw), fwd(0)), pl.BlockSpec((tb, vw), bwd(0)), state_spec],
        out_shape=[jax.ShapeDtypeStruct((batch * seq, vw), F32),
                   jax.ShapeDtypeStruct((batch * seq, vw), F32),
                   jax.ShapeDtypeStruct((batch, 2, GLA_HEADS, GLA_DV, GLA_DK), F32)],
        compiler_params=_params("arbitrary", "arbitrary"),
        name="gla",
    )(p, p, p, lr, p, p, p, lr, w_up, b_gate, s0)


def _cmlp_kernel(u_ref, v_ref, lng_ref, lnb_ref, ws_ref, bs_ref, o_ref, *, n_chunks):
    gv = _gelu(v_ref[...].astype(F32))
    mu = jnp.mean(gv, axis=-1, keepdims=True)
    dv = gv - mu
    var = jnp.mean(dv * dv, axis=-1, keepdims=True)
    vn = (dv * lax.rsqrt(var + EPS) * lng_ref[...] + lnb_ref[...]).astype(BF16)
    for ch in range(n_chunks):
        rows = slice(ch * CMLP_CHUNK, (ch + 1) * CMLP_CHUNK)
        for g in range(CMLP_GROUPS):
            cols = slice(g * CMLP_GROUP_DIM, (g + 1) * CMLP_GROUP_DIM)
            s = jnp.dot(ws_ref[g], vn[rows, cols], preferred_element_type=F32) + bs_ref[:, g:g + 1]
            o_ref[rows, cols] = (_gelu(u_ref[rows, cols].astype(F32)) * s).astype(o_ref.dtype)


def _cmlp(p, ln_g, ln_b, w_s, b_st):
    t = p.shape[0]
    tm = min(512, t)
    cu_blk = (2 * GLA_KEY_WIDTH + 2 * GLA_WIDTH) // CMLP_WIDTH
    return pl.pallas_call(
        functools.partial(_cmlp_kernel, n_chunks=tm // CMLP_CHUNK),
        grid=(t // tm,),
        in_specs=[pl.BlockSpec((tm, CMLP_WIDTH), lambda i: (i, cu_blk)),
                  pl.BlockSpec((tm, CMLP_WIDTH), lambda i: (i, cu_blk + 1)),
                  pl.BlockSpec((1, CMLP_WIDTH), lambda i: (0, 0)),
                  pl.BlockSpec((1, CMLP_WIDTH), lambda i: (0, 0)),
                  pl.BlockSpec((CMLP_GROUPS, CMLP_CHUNK, CMLP_CHUNK), lambda i: (0, 0, 0)),
                  pl.BlockSpec((CMLP_CHUNK, CMLP_GROUPS), lambda i: (0, 0))],
        out_specs=pl.BlockSpec((tm, CMLP_WIDTH), lambda i: (i, 0)),
        out_shape=jax.ShapeDtypeStruct((t, CMLP_WIDTH), BF16),
        compiler_params=_params("arbitrary"),
        name="cmlp",
    )(p, p, ln_g, ln_b, w_s, b_st)


def _outproj_kernel(of_ref, ob_ref, g_ref, cm_ref, x_ref, mod_ref, gng_ref, n2g_ref, w_ref,
                    h1_ref, xn2_ref):
    o = of_ref[...] + ob_ref[...]
    heads = []
    for h in range(GLA_HEADS):
        oh = o[:, h * GLA_DV:(h + 1) * GLA_DV]
        heads.append(oh * lax.rsqrt(jnp.mean(oh * oh, axis=-1, keepdims=True) + EPS))
    on = jnp.concatenate(heads, axis=-1) * gng_ref[...]
    y = (on * _silu(g_ref[...].astype(F32))).astype(BF16)
    mix = (jnp.dot(y, w_ref[0:GLA_WIDTH, :], preferred_element_type=F32)
           + jnp.dot(cm_ref[...], w_ref[GLA_WIDTH:GLA_WIDTH + CMLP_WIDTH, :], preferred_element_type=F32))
    h1 = x_ref[...] + mod_ref[0, 2:3, :] * mix
    h1_ref[...] = h1
    hn = h1 * lax.rsqrt(jnp.mean(h1 * h1, axis=-1, keepdims=True) + EPS) * n2g_ref[...]
    xn2_ref[...] = (hn * (1.0 + mod_ref[0, 4:5, :]) + mod_ref[0, 3:4, :]).astype(xn2_ref.dtype)


def _outproj(o_f, o_b, p, cm, x2, mod3, gla_norm_g, norm2_g, w_out, seq):
    t = x2.shape[0]
    tm = min(512, t)
    g_blk = (2 * GLA_KEY_WIDTH + GLA_WIDTH) // GLA_WIDTH
    return pl.pallas_call(
        _outproj_kernel,
        grid=(t // tm,),
        in_specs=[pl.BlockSpec((tm, GLA_WIDTH), lambda i: (i, 0)),
                  pl.BlockSpec((tm, GLA_WIDTH), lambda i: (i, 0)),
                  pl.BlockSpec((tm, GLA_WIDTH), lambda i: (i, g_blk)),
                  pl.BlockSpec((tm, CMLP_WIDTH), lambda i: (i, 0)),
                  pl.BlockSpec((tm, D_MODEL), lambda i: (i, 0)),
                  pl.BlockSpec((1, N_MOD, D_MODEL), lambda i: ((i * tm) // seq, 0, 0)),
                  pl.BlockSpec((1, GLA_WIDTH), lambda i: (0, 0)),
                  pl.BlockSpec((1, D_MODEL), lambda i: (0, 0)),
                  pl.BlockSpec((GLA_WIDTH + CMLP_WIDTH, D_MODEL), lambda i: (0, 0))],
        out_specs=[pl.BlockSpec((tm, D_MODEL), lambda i: (i, 0)),
                   pl.BlockSpec((tm, D_MODEL), lambda i: (i, 0))],
        out_shape=[jax.ShapeDtypeStruct((t, D_MODEL), F32),
                   jax.ShapeDtypeStruct((t, D_MODEL), BF16)],
        compiler_params=_params("arbitrary"),
        name="outproj",
    )(o_f, o_b, p, cm, x2, mod3, gla_norm_g, norm2_g, w_out)


def _top16_exact(s, key_iota):
    neg = jnp.float32(-jnp.inf)
    vals = []
    rank = jnp.full(s.shape, float(PEER_TOPK), F32)
    for r in range(PEER_TOPK):
        m = jnp.max(s, axis=0, keepdims=True)
        idx = jnp.min(jnp.where(s == m, key_iota, float(s.shape[0])), axis=0, keepdims=True)
        sel = key_iota == idx
        rank = jnp.where(sel, float(r), rank)
        s = jnp.where(sel, neg, s)
        vals.append(m)
    return jnp.concatenate(vals, axis=0), rank


SUBLANES = 8


def _key_groups(s):
    return [s[q * SUBLANES:(q + 1) * SUBLANES] for q in range(s.shape[0] // SUBLANES)]


def _oddeven_mergesort_pairs(n):
    pairs = []

    def merge(lo, hi, r):
        step = r * 2
        if step < hi - lo:
            merge(lo, hi, step)
            merge(lo + r, hi, step)
            pairs.extend((i, i + r) for i in range(lo + r, hi - r, step))
        else:
            pairs.append((lo, lo + r))

    def sort(lo, hi):
        if hi - lo >= 1:
            mid = lo + (hi - lo) // 2
            sort(lo, mid)
            sort(mid + 1, hi)
            merge(lo, hi, 1)

    sort(0, n - 1)
    return pairs


def _sorted_top16(s):
    k = PEER_TOPK
    groups = _key_groups(s)
    assert len(groups) <= k
    rows = groups + [None] * (k - len(groups))

    def vmax(a, b):
        return b if a is None else a if b is None else jnp.maximum(a, b)

    def exchange(i, j):
        if rows[j] is None:
            return
        if rows[i] is None:
            rows[i], rows[j] = rows[j], None
        else:
            rows[i], rows[j] = jnp.maximum(rows[i], rows[j]), jnp.minimum(rows[i], rows[j])

    for i, j in _oddeven_mergesort_pairs(k):
        exchange(i, j)
    shift = SUBLANES // 2
    while shift:
        other = [None if r is None else pltpu.roll(r, shift, 0) for r in rows]
        rows = [vmax(rows[i], other[k - 1 - i]) for i in range(k)]
        d = k // 2
        while d:
            for i in range(k):
                if not i & d:
                    exchange(i, i + d)
            d //= 2
        shift //= 2
    reach = sum(jnp.where(g >= rows[k - 1], 1.0, 0.0) for g in groups)
    repeats = sum(jnp.where(rows[b] == rows[b + 1], 1.0, 0.0) for b in range(k - 1))
    taken = jnp.sum(reach, axis=0, keepdims=True) + float(k) * repeats[0:1]
    return rows, taken


_CAND_INVALID = 1 << 20
_CAND_ROWS = 72


def _staircase_positions():
    assert PEER_TOPK == 16
    k, inv = PEER_TOPK, _CAND_INVALID
    pos = list(range(k))
    pos += [inv] + [a * k for a in range(1, k)]
    pos += [inv] + [k + b for b in range(1, 8)]
    pos += [inv, inv] + [a * k + 1 for a in range(2, 8)]
    pos += [2 * k + b if b in (2, 3, 4) else inv for b in range(8)]
    pos += [3 * k + b if b in (2, 3) else inv for b in range(8)]
    pos += [4 * k + b if b == 2 else inv for b in range(8)]
    assert len(pos) == _CAND_ROWS and sum(p != inv for p in pos) == 50
    return jnp.broadcast_to(jnp.array(pos, F32)[:, None], (_CAND_ROWS, ROUTE_LANES))


ROUTE_LANES = 128
ROUTE_TILES_PER_ITER = 4


def _route_kernel(xn_ref, wqt_ref, sk_ref, pos_ref, e1_ref, c1_ref, e2_ref, r2_ref, qt_ref):
    qt_ref[...] = lax.dot_general(wqt_ref[...], xn_ref[...], _NT, preferred_element_type=F32)
    n = ROUTE_LANES
    n_tiles = xn_ref.shape[0] // n
    k = PEER_TOPK
    key_iota = lax.broadcasted_iota(jnp.int32, (PEER_NKEYS, n), 0).astype(F32)
    row16 = lax.broadcasted_iota(jnp.int32, (k, n), 0)
    neg = jnp.float32(-jnp.inf)
    invalid = float(_CAND_INVALID)

    def tiles(t, carry):
        first = t * ROUTE_TILES_PER_ITER
        taken = [tile(first + u, exact=False) for u in range(ROUTE_TILES_PER_ITER)]

        @pl.when(jnp.max(functools.reduce(jnp.maximum, taken)) > float(k))
        def _():
            for u in range(ROUTE_TILES_PER_ITER):
                tile(first + u, exact=True)
        return carry

    def tile(t, exact):
        h = t // n_tiles
        lanes = pl.ds(pl.multiple_of((t % n_tiles) * n, n), n)
        base = pl.multiple_of(h * PEER_QDIM, PEER_QDIM)
        q1 = qt_ref[pl.ds(base, PEER_HALF), lanes].astype(BF16)
        q2 = qt_ref[pl.ds(base + PEER_HALF, PEER_HALF), lanes].astype(BF16)
        s1 = jnp.dot(sk_ref[0, h], q1, preferred_element_type=F32)
        s2 = jnp.dot(sk_ref[1, h], q2, preferred_element_type=F32)
        if exact:
            v1, rank1 = _top16_exact(s1, key_iota)
            v2, rank2 = _top16_exact(s2, key_iota)
        else:
            rows1, taken1 = _sorted_top16(s1)
            rows2, taken2 = _sorted_top16(s2)
            v1 = jnp.concatenate([r[0:1] for r in rows1], axis=0)
            v2 = jnp.concatenate([r[0:1] for r in rows2], axis=0)
            ranks = []
            for piece in _key_groups(s2):
                rk = jnp.full(piece.shape, float(k), F32)
                for b in range(k):
                    rk = jnp.where(piece == rows2[b], float(b), rk)
                ranks.append(rk)
            rank2 = jnp.concatenate(ranks, axis=0)
        pos = pos_ref[...]
        cand = jnp.concatenate([v1[0:1] + v2, v1 + v2[0:1], v1[1:2] + v2[0:8], v1[0:8] + v2[1:2],
                                v1[2:3] + v2[0:8], v1[3:4] + v2[0:8], v1[4:5] + v2[0:8]], axis=0)
        cand = jnp.where(pos < invalid, cand, neg)
        best = cand[0:1, :]
        if exact:
            rem = cand
            for _ in range(k):
                m = jnp.max(rem, axis=0, keepdims=True)
                idx = jnp.min(jnp.where(rem == m, pos, invalid), axis=0, keepdims=True)
                rem = jnp.where(pos == idx, neg, rem)
            chosen = rem != cand
        else:
            best_pairs, taken_pairs = _sorted_top16(cand)
            chosen = jnp.concatenate([g >= best_pairs[k - 1] for g in _key_groups(cand)], axis=0)
        z = jnp.sum(jnp.where(chosen, jnp.exp(cand - best), 0.0), axis=0, keepdims=True)
        ch = chosen.astype(F32)
        rowsum = lambda lo, hi: jnp.sum(ch[lo:hi], axis=0, keepdims=True)
        cnt = ch[16:32] + jnp.concatenate([ch[40:48], jnp.zeros((8, n), F32)], axis=0)
        for a, (lo, hi) in enumerate(((0, 16), (32, 40), (48, 56), (56, 64), (64, 72))):
            cnt = cnt + jnp.where(row16 == a, rowsum(lo, hi), 0.0)
        c1 = jnp.zeros((PEER_NKEYS, n), F32)
        for a in range(k):
            is_rank_a = (rank1 == float(a)) if exact else (s1 == v1[a:a + 1])
            c1 = jnp.where(is_rank_a, cnt[a:a + 1], c1)
        e1_ref[h, :, lanes] = jnp.exp(s1 - v1[0:1, :]) * (1.0 / z)
        c1_ref[h, :, lanes] = c1
        e2_ref[h, :, lanes] = jnp.exp(s2 - v2[0:1, :]).astype(e2_ref.dtype)
        r2_ref[h, :, lanes] = rank2.astype(r2_ref.dtype)
        if exact:
            return None
        return jnp.maximum(jnp.maximum(taken1, taken2), taken_pairs)

    lax.fori_loop(0, PEER_HEADS * n_tiles // ROUTE_TILES_PER_ITER, tiles, 0)


def _route(xn2, wq_t, sk):
    t = xn2.shape[0]
    tr = min(512, t)
    tab = lambda dt: jax.ShapeDtypeStruct((PEER_HEADS, PEER_NKEYS, t), dt)
    tab_spec = pl.BlockSpec((PEER_HEADS, PEER_NKEYS, tr), lambda i: (0, 0, i))
    return pl.pallas_call(
        _route_kernel,
        grid=(t // tr,),
        in_specs=[pl.BlockSpec((tr, D_MODEL), lambda i: (i, 0)),
                  pl.BlockSpec((PEER_HEADS * PEER_QDIM, D_MODEL), lambda i: (0, 0)),
                  pl.BlockSpec((2, PEER_HEADS, PEER_NKEYS, PEER_HALF), lambda i: (0, 0, 0, 0)),
                  pl.BlockSpec((_CAND_ROWS, ROUTE_LANES), lambda i: (0, 0))],
        out_specs=[tab_spec] * 4,
        out_shape=[tab(F32), tab(F32), tab(BF16), tab(BF16)],
        scratch_shapes=[pltpu.VMEM((PEER_HEADS * PEER_QDIM, tr), F32)],
        compiler_params=_params("arbitrary"),
        name="route",
    )(xn2, wq_t, sk, _staircase_positions())


PEER_CHUNK = 512
PEER_CHUNKS_PER_STEP = 2
PEER_TOKENS = 512
PEER_WEIGHT_LANES = 256
BF16_ROWS = 16
PEER_VMEM_LIMIT_BYTES = 63 * 1024 * 1024


def _peer_kernel(xn_ref, e1_ref, c1_ref, e2_ref, r2_ref, u_ref, vt_ref, h1_ref, mod_ref, fg_ref,
                 o_ref, acc_ref, xs_ref):
    j = pl.program_id(1)
    ec = PEER_CHUNK
    n = xn_ref.shape[0]
    lanes = PEER_WEIGHT_LANES
    rows_per_chunk = ec // PEER_NKEYS
    rows_per_step = PEER_CHUNKS_PER_STEP * rows_per_chunk

    @pl.when(j == 0)
    def _():
        acc_ref[...] = jnp.zeros_like(acc_ref)
        xs_ref[...] = xn_ref[...]

    zero = jnp.zeros((), BF16)
    keys = pl.ds(pl.multiple_of(j * rows_per_step, rows_per_step), rows_per_step)
    chains = [(kk, tc) for kk in range(PEER_CHUNKS_PER_STEP) for tc in range(n // lanes)]

    def routing_weights(kk, tc):
        cols = slice(tc * lanes, (tc + 1) * lanes)
        row = (BF16_ROWS, lanes)
        reps = PEER_NKEYS // BF16_ROWS
        blocks = []
        for r in range(rows_per_chunk):
            i1 = kk * rows_per_chunk + r
            w = None
            for h in range(PEER_HEADS):
                e1 = jnp.broadcast_to(e1_ref[h, keys, cols][i1:i1 + 1], row).astype(BF16)
                c1 = jnp.broadcast_to(c1_ref[h, keys, cols][i1:i1 + 1], row).astype(BF16)
                e1 = jnp.concatenate([e1] * reps, axis=0)
                c1 = jnp.concatenate([c1] * reps, axis=0)
                t = e1 * jnp.where(r2_ref[h, :, cols] < c1, e2_ref[h, :, cols], zero)
                w = t if w is None else w + t
            blocks.append(w)
        return jnp.concatenate(blocks, axis=0)

    acts = [lax.dot_general(u_ref[kk * ec:(kk + 1) * ec, :], xs_ref[tc * lanes:(tc + 1) * lanes, :], _NT,
                            preferred_element_type=F32) for kk, tc in chains]
    for (kk, tc), act in zip(chains, acts):
        p = routing_weights(kk, tc) * _gelu(act).astype(BF16)
        acc_ref[:, tc * lanes:(tc + 1) * lanes] += jnp.dot(vt_ref[:, kk * ec:(kk + 1) * ec], p,
                                                           preferred_element_type=F32)

    @pl.when(j == pl.num_programs(1) - 1)
    def _():
        h2 = h1_ref[...] + mod_ref[0, 5:6, :] * acc_ref[...].T
        o_ref[...] = h2 * lax.rsqrt(jnp.mean(h2 * h2, axis=-1, keepdims=True) + EPS) * fg_ref[...]


def _peer(xn2, tabs, u, v_t, h1, mod3, final_g, seq):
    t = xn2.shape[0]
    tb = min(PEER_TOKENS, t)
    ecs = PEER_CHUNKS_PER_STEP * PEER_CHUNK
    tab_spec = pl.BlockSpec((PEER_HEADS, PEER_NKEYS, tb), lambda i, j: (0, 0, i))
    return pl.pallas_call(
        _peer_kernel,
        grid=(t // tb, PEER_EXPERTS // ecs),
        in_specs=[pl.BlockSpec((tb, D_MODEL), lambda i, j: (i, 0)),
                  tab_spec, tab_spec, tab_spec, tab_spec,
                  pl.BlockSpec((ecs, D_MODEL), lambda i, j: (j, 0)),
                  pl.BlockSpec((D_MODEL, ecs), lambda i, j: (0, j)),
                  pl.BlockSpec((tb, D_MODEL), lambda i, j: (i, 0)),
                  pl.BlockSpec((1, N_MOD, D_MODEL), lambda i, j: ((i * tb) // seq, 0, 0)),
                  pl.BlockSpec((1, D_MODEL), lambda i, j: (0, 0))],
        out_specs=pl.BlockSpec((tb, D_MODEL), lambda i, j: (i, 0)),
        out_shape=jax.ShapeDtypeStruct((t, D_MODEL), F32),
        scratch_shapes=[pltpu.VMEM((D_MODEL, tb), F32), pltpu.VMEM((tb, D_MODEL), BF16)],
        compiler_params=pltpu.CompilerParams(dimension_semantics=("arbitrary", "arbitrary"),
                                             vmem_limit_bytes=PEER_VMEM_LIMIT_BYTES),
        name="peer",
    )(xn2, *tabs, u, v_t, h1, mod3, final_g)


def kernel(x, c, ctx, c_ctx, norm1_g, norm2_g, w_mod, b_mod, w_in, w_gate_up, b_gate, gla_norm_g,
           cmlp_ln_g, cmlp_ln_b, w_spatial, b_spatial, w_out, peer_wq, peer_sub_keys, peer_u, peer_v,
           final_norm_g):
    batch, seq, d = x.shape
    ctx_len = ctx.shape[1]
    depth = w_mod.shape[0]
    assert d == D_MODEL and depth == 1 and batch + 1 <= MOD_ROWS
    assert seq % MAX_TOKEN_TILE == 0 and ctx_len % GLA_CHUNK == 0
    i = 0

    cc = jnp.zeros((MOD_ROWS, d), F32).at[:batch].set(c).at[batch].set(c_ctx)
    mod3 = _modulation(cc, w_mod[i], b_mod[i]).reshape(MOD_ROWS, N_MOD, d)

    qk_end = 2 * GLA_KEY_WIDTH + 2 * GLA_WIDTH
    lr_end = qk_end + 2 * GLA_GATE_RANK
    w = w_in[i].astype(BF16)
    w_main = jnp.concatenate([w[:, :qk_end], w[:, lr_end:]], axis=1)
    w_lr = jnp.pad(w[:, qk_end:lr_end], ((0, 0), (0, LR_PAD - 2 * GLA_GATE_RANK)))
    n1 = norm1_g[i].reshape(1, d)
    x2 = x.reshape(batch * seq, d)
    p_lat, lr_lat = _inproj(x2, mod3, n1, w_main, w_lr, lambda t, tm: (t * tm) // seq)
    p_ctx, lr_ctx = _inproj(ctx.reshape(batch * ctx_len, d), mod3, n1, w_main, w_lr, lambda t, tm: batch)

    w_up = jnp.zeros((2, LR_PAD, GLA_KEY_WIDTH), F32)
    for dd in range(2):
        w_up = w_up.at[dd, dd * GLA_GATE_RANK:(dd + 1) * GLA_GATE_RANK].set(w_gate_up[i, dd])
    w_up = w_up.astype(BF16)
    bg = b_gate[i].reshape(2, 1, GLA_KEY_WIDTH)
    s_zero = jnp.zeros((batch, 2, GLA_HEADS, GLA_DV, GLA_DK), F32)
    _, _, s_ctx = _gla(p_ctx, lr_ctx, w_up, bg, s_zero, batch, ctx_len)
    o_f, o_b, _ = _gla(p_lat, lr_lat, w_up, bg, s_ctx, batch, seq)

    cm = _cmlp(p_lat, cmlp_ln_g[i].reshape(1, -1), cmlp_ln_b[i].reshape(1, -1),
               w_spatial[i].astype(BF16), b_spatial[i].T)

    h1, xn2 = _outproj(o_f, o_b, p_lat, cm, x2, mod3, gla_norm_g[i].reshape(1, -1),
                       norm2_g[i].reshape(1, d), w_out[i].astype(BF16), seq)

    tabs = _route(xn2, peer_wq[i].T.astype(BF16), peer_sub_keys[i].astype(BF16))
    out = _peer(xn2, tabs, peer_u[i].astype(BF16), peer_v[i].T.astype(BF16), h1, mod3,
                final_norm_g.reshape(1, d), seq)
    return out.reshape(batch, seq, d)
```

```python
import functools

import jax
import jax.numpy as jnp
from jax import lax
from jax.experimental import pallas as pl
from jax.experimental.pallas import tpu as pltpu

F32 = jnp.float32
BF16 = jnp.bfloat16

D_MODEL = 2048
GLA_HEADS = 4
GLA_DK = 128
GLA_DV = 256
GLA_KEY_WIDTH = GLA_HEADS * GLA_DK
GLA_WIDTH = GLA_HEADS * GLA_DV
GLA_GATE_RANK = 16
GLA_GATE_TAU = 16.0
GLA_CHUNK = 64
CMLP_GROUPS = 8
CMLP_GROUP_DIM = 128
CMLP_WIDTH = CMLP_GROUPS * CMLP_GROUP_DIM
CMLP_CHUNK = 128
PEER_HEADS = 8
PEER_NKEYS = 128
PEER_EXPERTS = PEER_NKEYS * PEER_NKEYS
PEER_QDIM = 256
PEER_HALF = PEER_QDIM // 2
PEER_TOPK = 16
N_MOD = 6
EPS = 1e-6

MAIN_WIDTH = 2 * GLA_KEY_WIDTH + 2 * GLA_WIDTH + 2 * CMLP_WIDTH
LR_PAD = 128
MOD_ROWS = 8
MAX_TOKEN_TILE = 1024

VMEM_LIMIT_BYTES = 56 * 1024 * 1024

_NT = (((1,), (1,)), ((), ()))
_TN = (((0,), (0,)), ((), ()))


def _params(*sem):
    return pltpu.CompilerParams(dimension_semantics=sem, vmem_limit_bytes=VMEM_LIMIT_BYTES)


def _gelu(x):
    return 0.5 * x * (1.0 + lax.erf(x * (2.0 ** -0.5)))


def _silu(x):
    return x * jax.nn.sigmoid(x)


def _mod_kernel(cc_ref, w_ref, b_ref, o_ref):
    s = _silu(cc_ref[...])
    o_ref[...] = jnp.dot(s, w_ref[...], precision=lax.Precision.HIGHEST,
                         preferred_element_type=F32) + b_ref[...]


def _modulation(cc, w_mod, b_mod):
    n = w_mod.shape[1]
    tn = 1024
    return pl.pallas_call(
        _mod_kernel,
        grid=(n // tn,),
        in_specs=[pl.BlockSpec((MOD_ROWS, D_MODEL), lambda j: (0, 0)),
                  pl.BlockSpec((D_MODEL, tn), lambda j: (0, j)),
                  pl.BlockSpec((1, tn), lambda j: (0, j))],
        out_specs=pl.BlockSpec((MOD_ROWS, tn), lambda j: (0, j)),
        out_shape=jax.ShapeDtypeStruct((MOD_ROWS, n), F32),
        compiler_params=_params("arbitrary"),
        name="mod",
    )(cc, w_mod, b_mod.reshape(1, n))


def _inproj_kernel(x_ref, mod_ref, g_ref, w_ref, wlr_ref, p_ref, lr_ref, xn_ref):
    @pl.when(pl.program_id(1) == 0)
    def _():
        x = x_ref[...]
        y = x * lax.rsqrt(jnp.mean(x * x, axis=-1, keepdims=True) + EPS) * g_ref[...]
        xn = (y * (1.0 + mod_ref[0, 1:2, :]) + mod_ref[0, 0:1, :]).astype(BF16)
        xn_ref[...] = xn
        lr_ref[...] = jnp.dot(xn, wlr_ref[...], preferred_element_type=F32).astype(lr_ref.dtype)

    p_ref[...] = jnp.dot(xn_ref[...], w_ref[...], preferred_element_type=F32).astype(p_ref.dtype)


def _inproj(x2, mod3, norm_g, w_main, w_lr, mod_row_of_tile):
    t = x2.shape[0]
    tm = min(MAX_TOKEN_TILE, t)
    tn = 1024
    return pl.pallas_call(
        _inproj_kernel,
        grid=(t // tm, MAIN_WIDTH // tn),
        in_specs=[pl.BlockSpec((tm, D_MODEL), lambda i, j: (i, 0)),
                  pl.BlockSpec((1, N_MOD, D_MODEL), lambda i, j: (mod_row_of_tile(i, tm), 0, 0)),
                  pl.BlockSpec((1, D_MODEL), lambda i, j: (0, 0)),
                  pl.BlockSpec((D_MODEL, tn), lambda i, j: (0, j)),
                  pl.BlockSpec((D_MODEL, LR_PAD), lambda i, j: (0, 0))],
        out_specs=[pl.BlockSpec((tm, tn), lambda i, j: (i, j)),
                   pl.BlockSpec((tm, LR_PAD), lambda i, j: (i, 0))],
        out_shape=[jax.ShapeDtypeStruct((t, MAIN_WIDTH), BF16),
                   jax.ShapeDtypeStruct((t, LR_PAD), BF16)],
        scratch_shapes=[pltpu.VMEM((tm, D_MODEL), BF16)],
        compiler_params=_params("arbitrary", "arbitrary"),
        name="inproj",
    )(x2, mod3, norm_g, w_main, w_lr)


def _gla_kernel(qf_ref, kf_ref, vf_ref, lrf_ref, qb_ref, kb_ref, vb_ref, lrb_ref,
                wup_ref, bg_ref, s0_ref, of_ref, ob_ref, st_ref, *, n_chunks):
    @pl.when(pl.program_id(1) == 0)
    def _():
        st_ref[...] = s0_ref[...]

    c = GLA_CHUNK
    row = lax.broadcasted_iota(jnp.int32, (c, c), 0)
    col = lax.broadcasted_iota(jnp.int32, (c, c), 1)
    keep = (col <= row, col >= row)
    q_scale = GLA_DK ** -0.5
    dirs = ((qf_ref, kf_ref, vf_ref, lrf_ref, of_ref), (qb_ref, kb_ref, vb_ref, lrb_ref, ob_ref))

    for d, (q_ref, k_ref, v_ref, lr_ref, o_ref) in enumerate(dirs):
        order = range(n_chunks) if d == 0 else range(n_chunks - 1, -1, -1)
        chunk = lambda ci: slice(ci * c, (ci + 1) * c)
        z = jnp.dot(lr_ref[...], wup_ref[d], preferred_element_type=F32) + bg_ref[d]
        la = (jnp.minimum(z, 0.0) - jnp.log1p(jnp.exp(-jnp.abs(z)))) * (1.0 / GLA_GATE_TAU)
        tri = keep[d].astype(BF16)
        la_hi = la.astype(BF16)
        la_lo = (la - la_hi.astype(F32)).astype(BF16)
        cums = [jnp.dot(tri, la_hi[chunk(ci)], preferred_element_type=F32)
                + jnp.dot(tri, la_lo[chunk(ci)], preferred_element_type=F32)
                for ci in range(n_chunks)]
        tots = [cm[c - 1:c, :] if d == 0 else cm[0:1, :] for cm in cums]
        cum = jnp.concatenate(cums, axis=0)
        tot = jnp.concatenate([jnp.broadcast_to(t, (c, t.shape[1])) for t in tots], axis=0)
        kf = k_ref[...].astype(F32)
        qd = (q_ref[...].astype(F32) * q_scale * jnp.exp(cum)).astype(BF16)
        kd = (kf * jnp.exp(-cum)).astype(BF16)
        k2 = (kf * jnp.exp(tot - cum)).astype(BF16)
        etots = [jnp.exp(t) for t in tots]
        for h in range(GLA_HEADS):
            ks = slice(h * GLA_DK, (h + 1) * GLA_DK)
            vs = slice(h * GLA_DV, (h + 1) * GLA_DV)
            intra, incr = [], []
            for ci in range(n_chunks):
                rows = chunk(ci)
                vh = v_ref[rows, vs]
                sc = lax.dot_general(qd[rows, ks], kd[rows, ks], _NT, preferred_element_type=F32)
                sc = jnp.where(keep[d], sc, 0.0).astype(BF16)
                intra.append(jnp.dot(sc, vh, preferred_element_type=F32))
                incr.append(lax.dot_general(vh, k2[rows, ks], _TN, preferred_element_type=F32))
            st = st_ref[0, d, h]
            for ci in order:
                rows = chunk(ci)
                o_ref[rows, vs] = intra[ci] + lax.dot_general(qd[rows, ks], st.astype(BF16), _NT,
                                                              preferred_element_type=F32)
                st = st * etots[ci][:, ks] + incr[ci]
            st_ref[0, d, h] = st


def _gla(p, lr, w_up, b_gate, s0, batch, seq):
    tb = min(512, seq)
    nb = seq // tb
    qw, vw = GLA_KEY_WIDTH, GLA_WIDTH
    fwd = lambda col: (lambda b, s: (b * nb + s, col))
    bwd = lambda col: (lambda b, s: (b * nb + nb - 1 - s, col))
    state_spec = pl.BlockSpec((1, 2, GLA_HEADS, GLA_DV, GLA_DK), lambda b, s: (b, 0, 0, 0, 0))
    in_specs = []
    for m in (fwd, bwd):
        in_specs += [pl.BlockSpec((tb, qw), m(0)), pl.BlockSpec((tb, qw), m(1)),
                     pl.BlockSpec((tb, vw), m(1)), pl.BlockSpec((tb, LR_PAD), m(0))]
    in_specs += [pl.BlockSpec((2, LR_PAD, qw), lambda b, s: (0, 0, 0)),
                 pl.BlockSpec((2, 1, qw), lambda b, s: (0, 0, 0)),
                 state_spec]
    return pl.pallas_call(
        functools.partial(_gla_kernel, n_chunks=tb // GLA_CHUNK),
        grid=(batch, nb),
        in_specs=in_specs,
        out_specs=[pl.BlockSpec((tb, vw), fwd(0)), pl.BlockSpec((tb, vw), bwd(0)), state_spec],
        out_shape=[jax.ShapeDtypeStruct((batch * seq, vw), F32),
                   jax.ShapeDtypeStruct((batch * seq, vw), F32),
                   jax.ShapeDtypeStruct((batch, 2, GLA_HEADS, GLA_DV, GLA_DK), F32)],
        compiler_params=_params("arbitrary", "arbitrary"),
        name="gla",
    )(p, p, p, lr, p, p, p, lr, w_up, b_gate, s0)


def _cmlp_kernel(u_ref, v_ref, lng_ref, lnb_ref, ws_ref, bs_ref, o_ref, *, n_chunks):
    gv = _gelu(v_ref[...].astype(F32))
    mu = jnp.mean(gv, axis=-1, keepdims=True)
    dv = gv - mu
    var = jnp.mean(dv * dv, axis=-1, keepdims=True)
    vn = (dv * lax.rsqrt(var + EPS) * lng_ref[...] + lnb_ref[...]).astype(BF16)
    for ch in range(n_chunks):
        rows = slice(ch * CMLP_CHUNK, (ch + 1) * CMLP_CHUNK)
        for g in range(CMLP_GROUPS):
            cols = slice(g * CMLP_GROUP_DIM, (g + 1) * CMLP_GROUP_DIM)
            s = jnp.dot(ws_ref[g], vn[rows, cols], preferred_element_type=F32) + bs_ref[:, g:g + 1]
            o_ref[rows, cols] = (_gelu(u_ref[rows, cols].astype(F32)) * s).astype(o_ref.dtype)


def _outproj_kernel(of_ref, ob_ref, g_ref, u_ref, v_ref, x_ref, mod_ref, gng_ref, n2g_ref,
                    lng_ref, lnb_ref, ws_ref, bs_ref, w_ref, h1_ref, xn2_ref, cm_ref):
    _cmlp_kernel(u_ref, v_ref, lng_ref, lnb_ref, ws_ref, bs_ref, cm_ref,
                 n_chunks=cm_ref.shape[0] // CMLP_CHUNK)
    o = of_ref[...] + ob_ref[...]
    heads = []
    for h in range(GLA_HEADS):
        oh = o[:, h * GLA_DV:(h + 1) * GLA_DV]
        heads.append(oh * lax.rsqrt(jnp.mean(oh * oh, axis=-1, keepdims=True) + EPS))
    on = jnp.concatenate(heads, axis=-1) * gng_ref[...]
    y = (on * _silu(g_ref[...].astype(F32))).astype(BF16)
    mix = (jnp.dot(y, w_ref[0:GLA_WIDTH, :], preferred_element_type=F32)
           + jnp.dot(cm_ref[...], w_ref[GLA_WIDTH:GLA_WIDTH + CMLP_WIDTH, :], preferred_element_type=F32))
    h1 = x_ref[...] + mod_ref[0, 2:3, :] * mix
    h1_ref[...] = h1
    hn = h1 * lax.rsqrt(jnp.mean(h1 * h1, axis=-1, keepdims=True) + EPS) * n2g_ref[...]
    xn2_ref[...] = (hn * (1.0 + mod_ref[0, 4:5, :]) + mod_ref[0, 3:4, :]).astype(xn2_ref.dtype)


def _outproj(o_f, o_b, p, x2, mod3, gla_norm_g, norm2_g, ln_g, ln_b, w_s, b_st, w_out, seq):
    t = x2.shape[0]
    tm = min(512, t)
    g_blk = (2 * GLA_KEY_WIDTH + GLA_WIDTH) // GLA_WIDTH
    cu_blk = (2 * GLA_KEY_WIDTH + 2 * GLA_WIDTH) // CMLP_WIDTH
    return pl.pallas_call(
        _outproj_kernel,
        grid=(t // tm,),
        in_specs=[pl.BlockSpec((tm, GLA_WIDTH), lambda i: (i, 0)),
                  pl.BlockSpec((tm, GLA_WIDTH), lambda i: (i, 0)),
                  pl.BlockSpec((tm, GLA_WIDTH), lambda i: (i, g_blk)),
                  pl.BlockSpec((tm, CMLP_WIDTH), lambda i: (i, cu_blk)),
                  pl.BlockSpec((tm, CMLP_WIDTH), lambda i: (i, cu_blk + 1)),
                  pl.BlockSpec((tm, D_MODEL), lambda i: (i, 0)),
                  pl.BlockSpec((1, N_MOD, D_MODEL), lambda i: ((i * tm) // seq, 0, 0)),
                  pl.BlockSpec((1, GLA_WIDTH), lambda i: (0, 0)),
                  pl.BlockSpec((1, D_MODEL), lambda i: (0, 0)),
                  pl.BlockSpec((1, CMLP_WIDTH), lambda i: (0, 0)),
                  pl.BlockSpec((1, CMLP_WIDTH), lambda i: (0, 0)),
                  pl.BlockSpec((CMLP_GROUPS, CMLP_CHUNK, CMLP_CHUNK), lambda i: (0, 0, 0)),
                  pl.BlockSpec((CMLP_CHUNK, CMLP_GROUPS), lambda i: (0, 0)),
                  pl.BlockSpec((GLA_WIDTH + CMLP_WIDTH, D_MODEL), lambda i: (0, 0))],
        out_specs=[pl.BlockSpec((tm, D_MODEL), lambda i: (i, 0)),
                   pl.BlockSpec((tm, D_MODEL), lambda i: (i, 0))],
        out_shape=[jax.ShapeDtypeStruct((t, D_MODEL), F32),
                   jax.ShapeDtypeStruct((t, D_MODEL), BF16)],
        scratch_shapes=[pltpu.VMEM((tm, CMLP_WIDTH), BF16)],
        compiler_params=_params("arbitrary"),
        name="outproj",
    )(o_f, o_b, p, p, p, x2, mod3, gla_norm_g, norm2_g, ln_g, ln_b, w_s, b_st, w_out)


def _top16_exact(s, key_iota):
    neg = jnp.float32(-jnp.inf)
    vals = []
    rank = jnp.full(s.shape, float(PEER_TOPK), F32)
    for r in range(PEER_TOPK):
        m = jnp.max(s, axis=0, keepdims=True)
        idx = jnp.min(jnp.where(s == m, key_iota, float(s.shape[0])), axis=0, keepdims=True)
        sel = key_iota == idx
        rank = jnp.where(sel, float(r), rank)
        s = jnp.where(sel, neg, s)
        vals.append(m)
    return jnp.concatenate(vals, axis=0), rank


SUBLANES = 8


def _key_groups(s):
    return [s[q * SUBLANES:(q + 1) * SUBLANES] for q in range(s.shape[0] // SUBLANES)]


def _oddeven_mergesort_pairs(n):
    pairs = []

    def merge(lo, hi, r):
        step = r * 2
        if step < hi - lo:
            merge(lo, hi, step)
            merge(lo + r, hi, step)
            pairs.extend((i, i + r) for i in range(lo + r, hi - r, step))
        else:
            pairs.append((lo, lo + r))

    def sort(lo, hi):
        if hi - lo >= 1:
            mid = lo + (hi - lo) // 2
            sort(lo, mid)
            sort(mid + 1, hi)
            merge(lo, hi, 1)

    sort(0, n - 1)
    return pairs


def _sorted_top16(s):
    k = PEER_TOPK
    groups = _key_groups(s)
    assert len(groups) <= k
    rows = groups + [None] * (k - len(groups))

    def vmax(a, b):
        return b if a is None else a if b is None else jnp.maximum(a, b)

    def exchange(i, j):
        if rows[j] is None:
            return
        if rows[i] is None:
            rows[i], rows[j] = rows[j], None
        else:
            rows[i], rows[j] = jnp.maximum(rows[i], rows[j]), jnp.minimum(rows[i], rows[j])

    for i, j in _oddeven_mergesort_pairs(k):
        exchange(i, j)
    shift = SUBLANES // 2
    while shift:
        other = [None if r is None else pltpu.roll(r, shift, 0) for r in rows]
        rows = [vmax(rows[i], other[k - 1 - i]) for i in range(k)]
        d = k // 2
        while d:
            for i in range(k):
                if not i & d:
                    exchange(i, i + d)
            d //= 2
        shift //= 2
    reach = sum(jnp.where(g >= rows[k - 1], 1.0, 0.0) for g in groups)
    repeats = sum(jnp.where(rows[b] == rows[b + 1], 1.0, 0.0) for b in range(k - 1))
    taken = jnp.sum(reach, axis=0, keepdims=True) + float(k) * repeats[0:1]
    return rows, taken


_CAND_INVALID = 1 << 20
_CAND_ROWS = 72


def _staircase_positions():
    assert PEER_TOPK == 16
    k, inv = PEER_TOPK, _CAND_INVALID
    pos = list(range(k))
    pos += [inv] + [a * k for a in range(1, k)]
    pos += [inv] + [k + b for b in range(1, 8)]
    pos += [inv, inv] + [a * k + 1 for a in range(2, 8)]
    pos += [2 * k + b if b in (2, 3, 4) else inv for b in range(8)]
    pos += [3 * k + b if b in (2, 3) else inv for b in range(8)]
    pos += [4 * k + b if b == 2 else inv for b in range(8)]
    assert len(pos) == _CAND_ROWS and sum(p != inv for p in pos) == 50
    return jnp.broadcast_to(jnp.array(pos, F32)[:, None], (_CAND_ROWS, ROUTE_LANES))


ROUTE_LANES = 128
ROUTE_TILES_PER_ITER = 4


def _route_kernel(xn_ref, wqt_ref, sk_ref, pos_ref, e1_ref, c1_ref, e2_ref, r2_ref, qt_ref):
    qt_ref[...] = lax.dot_general(wqt_ref[...], xn_ref[...], _NT, preferred_element_type=F32)
    n = ROUTE_LANES
    n_tiles = xn_ref.shape[0] // n
    k = PEER_TOPK
    key_iota = lax.broadcasted_iota(jnp.int32, (PEER_NKEYS, n), 0).astype(F32)
    row16 = lax.broadcasted_iota(jnp.int32, (k, n), 0)
    neg = jnp.float32(-jnp.inf)
    invalid = float(_CAND_INVALID)

    def tiles(t, carry):
        first = t * ROUTE_TILES_PER_ITER
        taken = [tile(first + u, exact=False) for u in range(ROUTE_TILES_PER_ITER)]

        @pl.when(jnp.max(functools.reduce(jnp.maximum, taken)) > float(k))
        def _():
            for u in range(ROUTE_TILES_PER_ITER):
                tile(first + u, exact=True)
        return carry

    def tile(t, exact):
        h = t // n_tiles
        lanes = pl.ds(pl.multiple_of((t % n_tiles) * n, n), n)
        base = pl.multiple_of(h * PEER_QDIM, PEER_QDIM)
        q1 = qt_ref[pl.ds(base, PEER_HALF), lanes].astype(BF16)
        q2 = qt_ref[pl.ds(base + PEER_HALF, PEER_HALF), lanes].astype(BF16)
        s1 = jnp.dot(sk_ref[0, h], q1, preferred_element_type=F32)
        s2 = jnp.dot(sk_ref[1, h], q2, preferred_element_type=F32)
        if exact:
            v1, rank1 = _top16_exact(s1, key_iota)
            v2, rank2 = _top16_exact(s2, key_iota)
        else:
            rows1, taken1 = _sorted_top16(s1)
            rows2, taken2 = _sorted_top16(s2)
            v1 = jnp.concatenate([r[0:1] for r in rows1], axis=0)
            v2 = jnp.concatenate([r[0:1] for r in rows2], axis=0)
            ranks = []
            for piece in _key_groups(s2):
                rk = jnp.full(piece.shape, float(k), F32)
                for b in range(k):
                    rk = jnp.where(piece == rows2[b], float(b), rk)
                ranks.append(rk)
            rank2 = jnp.concatenate(ranks, axis=0)
        pos = pos_ref[...]
        cand = jnp.concatenate([v1[0:1] + v2, v1 + v2[0:1], v1[1:2] + v2[0:8], v1[0:8] + v2[1:2],
                                v1[2:3] + v2[0:8], v1[3:4] + v2[0:8], v1[4:5] + v2[0:8]], axis=0)
        cand = jnp.where(pos < invalid, cand, neg)
        best = cand[0:1, :]
        if exact:
            rem = cand
            for _ in range(k):
                m = jnp.max(rem, axis=0, keepdims=True)
                idx = jnp.min(jnp.where(rem == m, pos, invalid), axis=0, keepdims=True)
                rem = jnp.where(pos == idx, neg, rem)
            chosen = rem != cand
        else:
            best_pairs, taken_pairs = _sorted_top16(cand)
            chosen = jnp.concatenate([g >= best_pairs[k - 1] for g in _key_groups(cand)], axis=0)
        z = jnp.sum(jnp.where(chosen, jnp.exp(cand - best), 0.0), axis=0, keepdims=True)
        ch = chosen.astype(F32)
        rowsum = lambda lo, hi: jnp.sum(ch[lo:hi], axis=0, keepdims=True)
        cnt = ch[16:32] + jnp.concatenate([ch[40:48], jnp.zeros((8, n), F32)], axis=0)
        for a, (lo, hi) in enumerate(((0, 16), (32, 40), (48, 56), (56, 64), (64, 72))):
            cnt = cnt + jnp.where(row16 == a, rowsum(lo, hi), 0.0)
        c1 = jnp.zeros((PEER_NKEYS, n), F32)
        for a in range(k):
            is_rank_a = (rank1 == float(a)) if exact else (s1 == v1[a:a + 1])
            c1 = jnp.where(is_rank_a, cnt[a:a + 1], c1)
        e1_ref[h, :, lanes] = jnp.exp(s1 - v1[0:1, :]) * (1.0 / z)
        c1_ref[h, :, lanes] = c1
        e2_ref[h, :, lanes] = jnp.exp(s2 - v2[0:1, :]).astype(e2_ref.dtype)
        r2_ref[h, :, lanes] = rank2.astype(r2_ref.dtype)
        if exact:
            return None
        return jnp.maximum(jnp.maximum(taken1, taken2), taken_pairs)

    lax.fori_loop(0, PEER_HEADS * n_tiles // ROUTE_TILES_PER_ITER, tiles, 0)


def _route(xn2, wq_t, sk):
    t = xn2.shape[0]
    tr = min(512, t)
    tab = lambda dt: jax.ShapeDtypeStruct((PEER_HEADS, PEER_NKEYS, t), dt)
    tab_spec = pl.BlockSpec((PEER_HEADS, PEER_NKEYS, tr), lambda i: (0, 0, i))
    return pl.pallas_call(
        _route_kernel,
        grid=(t // tr,),
        in_specs=[pl.BlockSpec((tr, D_MODEL), lambda i: (i, 0)),
                  pl.BlockSpec((PEER_HEADS * PEER_QDIM, D_MODEL), lambda i: (0, 0)),
                  pl.BlockSpec((2, PEER_HEADS, PEER_NKEYS, PEER_HALF), lambda i: (0, 0, 0, 0)),
                  pl.BlockSpec((_CAND_ROWS, ROUTE_LANES), lambda i: (0, 0))],
        out_specs=[tab_spec] * 4,
        out_shape=[tab(F32), tab(F32), tab(BF16), tab(BF16)],
        scratch_shapes=[pltpu.VMEM((PEER_HEADS * PEER_QDIM, tr), F32)],
        compiler_params=_params("arbitrary"),
        name="route",
    )(xn2, wq_t, sk, _staircase_positions())


PEER_CHUNK = 512
PEER_CHUNKS_PER_STEP = 2
PEER_TOKENS = 512
PEER_WEIGHT_LANES = 256
BF16_ROWS = 16
PEER_VMEM_LIMIT_BYTES = 63 * 1024 * 1024


def _peer_kernel(xn_ref, e1_ref, c1_ref, e2_ref, r2_ref, u_ref, vt_ref, h1_ref, mod_ref, fg_ref,
                 o_ref, acc_ref, xs_ref):
    j = pl.program_id(1)
    ec = PEER_CHUNK
    n = xn_ref.shape[0]
    lanes = PEER_WEIGHT_LANES
    rows_per_chunk = ec // PEER_NKEYS
    rows_per_step = PEER_CHUNKS_PER_STEP * rows_per_chunk

    @pl.when(j == 0)
    def _():
        acc_ref[...] = jnp.zeros_like(acc_ref)
        xs_ref[...] = xn_ref[...]

    zero = jnp.zeros((), BF16)
    keys = pl.ds(pl.multiple_of(j * rows_per_step, rows_per_step), rows_per_step)
    chains = [(kk, tc) for kk in range(PEER_CHUNKS_PER_STEP) for tc in range(n // lanes)]

    def routing_weights(kk, tc):
        cols = slice(tc * lanes, (tc + 1) * lanes)
        row = (BF16_ROWS, lanes)
        reps = PEER_NKEYS // BF16_ROWS
        blocks = []
        for r in range(rows_per_chunk):
            i1 = kk * rows_per_chunk + r
            w = None
            for h in range(PEER_HEADS):
                e1 = jnp.broadcast_to(e1_ref[h, keys, cols][i1:i1 + 1], row).astype(BF16)
                c1 = jnp.broadcast_to(c1_ref[h, keys, cols][i1:i1 + 1], row).astype(BF16)
                e1 = jnp.concatenate([e1] * reps, axis=0)
                c1 = jnp.concatenate([c1] * reps, axis=0)
                t = e1 * jnp.where(r2_ref[h, :, cols] < c1, e2_ref[h, :, cols], zero)
                w = t if w is None else w + t
            blocks.append(w)
        return jnp.concatenate(blocks, axis=0)

    acts = [lax.dot_general(u_ref[kk * ec:(kk + 1) * ec, :], xs_ref[tc * lanes:(tc + 1) * lanes, :], _NT,
                            preferred_element_type=F32) for kk, tc in chains]
    for (kk, tc), act in zip(chains, acts):
        p = routing_weights(kk, tc) * _gelu(act).astype(BF16)
        acc_ref[:, tc * lanes:(tc + 1) * lanes] += jnp.dot(vt_ref[:, kk * ec:(kk + 1) * ec], p,
                                                           preferred_element_type=F32)

    @pl.when(j == pl.num_programs(1) - 1)
    def _():
        h2 = h1_ref[...] + mod_ref[0, 5:6, :] * acc_ref[...].T
        o_ref[...] = h2 * lax.rsqrt(jnp.mean(h2 * h2, axis=-1, keepdims=True) + EPS) * fg_ref[...]


def _peer(xn2, tabs, u, v_t, h1, mod3, final_g, seq):
    t = xn2.shape[0]
    tb = min(PEER_TOKENS, t)
    ecs = PEER_CHUNKS_PER_STEP * PEER_CHUNK
    tab_spec = pl.BlockSpec((PEER_HEADS, PEER_NKEYS, tb), lambda i, j: (0, 0, i))
    return pl.pallas_call(
        _peer_kernel,
        grid=(t // tb, PEER_EXPERTS // ecs),
        in_specs=[pl.BlockSpec((tb, D_MODEL), lambda i, j: (i, 0)),
                  tab_spec, tab_spec, tab_spec, tab_spec,
                  pl.BlockSpec((ecs, D_MODEL), lambda i, j: (j, 0)),
                  pl.BlockSpec((D_MODEL, ecs), lambda i, j: (0, j)),
                  pl.BlockSpec((tb, D_MODEL), lambda i, j: (i, 0)),
                  pl.BlockSpec((1, N_MOD, D_MODEL), lambda i, j: ((i * tb) // seq, 0, 0)),
                  pl.BlockSpec((1, D_MODEL), lambda i, j: (0, 0))],
        out_specs=pl.BlockSpec((tb, D_MODEL), lambda i, j: (i, 0)),
        out_shape=jax.ShapeDtypeStruct((t, D_MODEL), F32),
        scratch_shapes=[pltpu.VMEM((D_MODEL, tb), F32), pltpu.VMEM((tb, D_MODEL), BF16)],
        compiler_params=pltpu.CompilerParams(dimension_semantics=("arbitrary", "arbitrary"),
                                             vmem_limit_bytes=PEER_VMEM_LIMIT_BYTES),
        name="peer",
    )(xn2, *tabs, u, v_t, h1, mod3, final_g)


def kernel(x, c, ctx, c_ctx, norm1_g, norm2_g, w_mod, b_mod, w_in, w_gate_up, b_gate, gla_norm_g,
           cmlp_ln_g, cmlp_ln_b, w_spatial, b_spatial, w_out, peer_wq, peer_sub_keys, peer_u, peer_v,
           final_norm_g):
    batch, seq, d = x.shape
    ctx_len = ctx.shape[1]
    depth = w_mod.shape[0]
    assert d == D_MODEL and depth == 1 and batch + 1 <= MOD_ROWS
    assert seq % MAX_TOKEN_TILE == 0 and ctx_len % GLA_CHUNK == 0
    i = 0

    cc = jnp.zeros((MOD_ROWS, d), F32).at[:batch].set(c).at[batch].set(c_ctx)
    mod3 = _modulation(cc, w_mod[i], b_mod[i]).reshape(MOD_ROWS, N_MOD, d)

    qk_end = 2 * GLA_KEY_WIDTH + 2 * GLA_WIDTH
    lr_end = qk_end + 2 * GLA_GATE_RANK
    w = w_in[i].astype(BF16)
    w_main = jnp.concatenate([w[:, :qk_end], w[:, lr_end:]], axis=1)
    w_lr = jnp.pad(w[:, qk_end:lr_end], ((0, 0), (0, LR_PAD - 2 * GLA_GATE_RANK)))
    n1 = norm1_g[i].reshape(1, d)
    x2 = x.reshape(batch * seq, d)
    p_lat, lr_lat = _inproj(x2, mod3, n1, w_main, w_lr, lambda t, tm: (t * tm) // seq)
    p_ctx, lr_ctx = _inproj(ctx.reshape(batch * ctx_len, d), mod3, n1, w_main, w_lr, lambda t, tm: batch)

    w_up = jnp.zeros((2, LR_PAD, GLA_KEY_WIDTH), F32)
    for dd in range(2):
        w_up = w_up.at[dd, dd * GLA_GATE_RANK:(dd + 1) * GLA_GATE_RANK].set(w_gate_up[i, dd])
    w_up = w_up.astype(BF16)
    bg = b_gate[i].reshape(2, 1, GLA_KEY_WIDTH)
    s_zero = jnp.zeros((batch, 2, GLA_HEADS, GLA_DV, GLA_DK), F32)
    _, _, s_ctx = _gla(p_ctx, lr_ctx, w_up, bg, s_zero, batch, ctx_len)
    o_f, o_b, _ = _gla(p_lat, lr_lat, w_up, bg, s_ctx, batch, seq)

    h1, xn2 = _outproj(o_f, o_b, p_lat, x2, mod3, gla_norm_g[i].reshape(1, -1), norm2_g[i].reshape(1, d),
                       cmlp_ln_g[i].reshape(1, -1), cmlp_ln_b[i].reshape(1, -1),
                       w_spatial[i].astype(BF16), b_spatial[i].T, w_out[i].astype(BF16), seq)

    tabs = _route(xn2, peer_wq[i].T.astype(BF16), peer_sub_keys[i].astype(BF16))
    out = _peer(xn2, tabs, peer_u[i].astype(BF16), peer_v[i].T.astype(BF16), h1, mod3,
                final_norm_g.reshape(1, d), seq)
    return out.reshape(batch, seq, d)
```

```python
import functools

import jax
import jax.numpy as jnp
from jax import lax
from jax.experimental import pallas as pl
from jax.experimental.pallas import tpu as pltpu

F32 = jnp.float32
BF16 = jnp.bfloat16

D_MODEL = 2048
GLA_HEADS = 4
GLA_DK = 128
GLA_DV = 256
GLA_KEY_WIDTH = GLA_HEADS * GLA_DK
GLA_WIDTH = GLA_HEADS * GLA_DV
GLA_GATE_RANK = 16
GLA_GATE_TAU = 16.0
GLA_CHUNK = 64
CMLP_GROUPS = 8
CMLP_GROUP_DIM = 128
CMLP_WIDTH = CMLP_GROUPS * CMLP_GROUP_DIM
CMLP_CHUNK = 128
PEER_HEADS = 8
PEER_NKEYS = 128
PEER_EXPERTS = PEER_NKEYS * PEER_NKEYS
PEER_QDIM = 256
PEER_HALF = PEER_QDIM // 2
PEER_TOPK = 16
N_MOD = 6
EPS = 1e-6

MAIN_WIDTH = 2 * GLA_KEY_WIDTH + 2 * GLA_WIDTH + 2 * CMLP_WIDTH
LR_PAD = 128
MOD_ROWS = 8
MAX_TOKEN_TILE = 1024

VMEM_LIMIT_BYTES = 56 * 1024 * 1024

_NT = (((1,), (1,)), ((), ()))
_TN = (((0,), (0,)), ((), ()))


def _params(*sem):
    return pltpu.CompilerParams(dimension_semantics=sem, vmem_limit_bytes=VMEM_LIMIT_BYTES)


def _gelu(x):
    return 0.5 * x * (1.0 + lax.erf(x * (2.0 ** -0.5)))


def _silu(x):
    return x * jax.nn.sigmoid(x)


def _mod_kernel(cc_ref, w_ref, b_ref, o_ref):
    s = _silu(cc_ref[...])
    w = w_ref[...].astype(BF16)
    s_hi = s.astype(BF16)
    s_lo = (s - s_hi.astype(F32)).astype(BF16)
    o_ref[...] = (jnp.dot(s_hi, w, preferred_element_type=F32)
                  + jnp.dot(s_lo, w, preferred_element_type=F32) + b_ref[...])


def _modulation(cc, w_mod, b_mod):
    n = w_mod.shape[1]
    tn = 1024
    return pl.pallas_call(
        _mod_kernel,
        grid=(n // tn,),
        in_specs=[pl.BlockSpec((MOD_ROWS, D_MODEL), lambda j: (0, 0)),
                  pl.BlockSpec((D_MODEL, tn), lambda j: (0, j)),
                  pl.BlockSpec((1, tn), lambda j: (0, j))],
        out_specs=pl.BlockSpec((MOD_ROWS, tn), lambda j: (0, j)),
        out_shape=jax.ShapeDtypeStruct((MOD_ROWS, n), F32),
        compiler_params=_params("arbitrary"),
        name="mod",
    )(cc, w_mod, b_mod.reshape(1, n))


def _inproj_kernel(x_ref, mod_ref, g_ref, w_ref, wlr_ref, p_ref, lr_ref, xn_ref):
    @pl.when(pl.program_id(1) == 0)
    def _():
        x = x_ref[...]
        y = x * lax.rsqrt(jnp.mean(x * x, axis=-1, keepdims=True) + EPS) * g_ref[...]
        xn = (y * (1.0 + mod_ref[0, 1:2, :]) + mod_ref[0, 0:1, :]).astype(BF16)
        xn_ref[...] = xn
        lr_ref[...] = jnp.dot(xn, wlr_ref[...], preferred_element_type=F32).astype(lr_ref.dtype)

    p_ref[...] = jnp.dot(xn_ref[...], w_ref[...], preferred_element_type=F32).astype(p_ref.dtype)


def _inproj(x2, mod3, norm_g, w_main, w_lr, mod_row_of_tile):
    t = x2.shape[0]
    tm = min(MAX_TOKEN_TILE, t)
    tn = 1024
    return pl.pallas_call(
        _inproj_kernel,
        grid=(t // tm, MAIN_WIDTH // tn),
        in_specs=[pl.BlockSpec((tm, D_MODEL), lambda i, j: (i, 0)),
                  pl.BlockSpec((1, N_MOD, D_MODEL), lambda i, j: (mod_row_of_tile(i, tm), 0, 0)),
                  pl.BlockSpec((1, D_MODEL), lambda i, j: (0, 0)),
                  pl.BlockSpec((D_MODEL, tn), lambda i, j: (0, j)),
                  pl.BlockSpec((D_MODEL, LR_PAD), lambda i, j: (0, 0))],
        out_specs=[pl.BlockSpec((tm, tn), lambda i, j: (i, j)),
                   pl.BlockSpec((tm, LR_PAD), lambda i, j: (i, 0))],
        out_shape=[jax.ShapeDtypeStruct((t, MAIN_WIDTH), BF16),
                   jax.ShapeDtypeStruct((t, LR_PAD), BF16)],
        scratch_shapes=[pltpu.VMEM((tm, D_MODEL), BF16)],
        compiler_params=_params("arbitrary", "arbitrary"),
        name="inproj",
    )(x2, mod3, norm_g, w_main, w_lr)


def _gla_kernel(qf_ref, kf_ref, vf_ref, lrf_ref, qb_ref, kb_ref, vb_ref, lrb_ref,
                wup_ref, bg_ref, s0_ref, of_ref, ob_ref, st_ref, *, n_chunks):
    @pl.when(pl.program_id(1) == 0)
    def _():
        st_ref[...] = s0_ref[...]

    c = GLA_CHUNK
    row = lax.broadcasted_iota(jnp.int32, (c, c), 0)
    col = lax.broadcasted_iota(jnp.int32, (c, c), 1)
    keep = (col <= row, col >= row)
    q_scale = GLA_DK ** -0.5
    dirs = ((qf_ref, kf_ref, vf_ref, lrf_ref, of_ref), (qb_ref, kb_ref, vb_ref, lrb_ref, ob_ref))

    for d, (q_ref, k_ref, v_ref, lr_ref, o_ref) in enumerate(dirs):
        order = range(n_chunks) if d == 0 else range(n_chunks - 1, -1, -1)
        chunk = lambda ci: slice(ci * c, (ci + 1) * c)
        z = jnp.dot(lr_ref[...], wup_ref[d], preferred_element_type=F32) + bg_ref[d]
        la = (jnp.minimum(z, 0.0) - jnp.log1p(jnp.exp(-jnp.abs(z)))) * (1.0 / GLA_GATE_TAU)
        tri = keep[d].astype(BF16)
        la_hi = la.astype(BF16)
        la_lo = (la - la_hi.astype(F32)).astype(BF16)
        cums = [jnp.dot(tri, la_hi[chunk(ci)], preferred_element_type=F32)
                + jnp.dot(tri, la_lo[chunk(ci)], preferred_element_type=F32)
                for ci in range(n_chunks)]
        tots = [cm[c - 1:c, :] if d == 0 else cm[0:1, :] for cm in cums]
        cum = jnp.concatenate(cums, axis=0)
        tot = jnp.concatenate([jnp.broadcast_to(t, (c, t.shape[1])) for t in tots], axis=0)
        kf = k_ref[...].astype(F32)
        qd = (q_ref[...].astype(F32) * q_scale * jnp.exp(cum)).astype(BF16)
        kd = (kf * jnp.exp(-cum)).astype(BF16)
        k2 = (kf * jnp.exp(tot - cum)).astype(BF16)
        etots = [jnp.exp(t) for t in tots]
        for h in range(GLA_HEADS):
            ks = slice(h * GLA_DK, (h + 1) * GLA_DK)
            vs = slice(h * GLA_DV, (h + 1) * GLA_DV)
            intra, incr = [], []
            for ci in range(n_chunks):
                rows = chunk(ci)
                vh = v_ref[rows, vs]
                sc = lax.dot_general(qd[rows, ks], kd[rows, ks], _NT, preferred_element_type=F32)
                sc = jnp.where(keep[d], sc, 0.0).astype(BF16)
                intra.append(jnp.dot(sc, vh, preferred_element_type=F32))
                incr.append(lax.dot_general(vh, k2[rows, ks], _TN, preferred_element_type=F32))
            st = st_ref[0, d, h]
            for ci in order:
                rows = chunk(ci)
                o_ref[rows, vs] = intra[ci] + lax.dot_general(qd[rows, ks], st.astype(BF16), _NT,
                                                              preferred_element_type=F32)
                st = st * etots[ci][:, ks] + incr[ci]
            st_ref[0, d, h] = st


def _gla(p, lr, w_up, b_gate, s0, batch, seq):
    tb = min(512, seq)
    nb = seq // tb
    qw, vw = GLA_KEY_WIDTH, GLA_WIDTH
    fwd = lambda col: (lambda b, s: (b * nb + s, col))
    bwd = lambda col: (lambda b, s: (b * nb + nb - 1 - s, col))
    state_spec = pl.BlockSpec((1, 2, GLA_HEADS, GLA_DV, GLA_DK), lambda b, s: (b, 0, 0, 0, 0))
    in_specs = []
    for m in (fwd, bwd):
        in_specs += [pl.BlockSpec((tb, qw), m(0)), pl.BlockSpec((tb, qw), m(1)),
                     pl.BlockSpec((tb, vw), m(1)), pl.BlockSpec((tb, LR_PAD), m(0))]
    in_specs += [pl.BlockSpec((2, LR_PAD, qw), lambda b, s: (0, 0, 0)),
                 pl.BlockSpec((2, 1, qw), lambda b, s: (0, 0, 0)),
                 state_spec]
    return pl.pallas_call(
        functools.partial(_gla_kernel, n_chunks=tb // GLA_CHUNK),
        grid=(batch, nb),
        in_specs=in_specs,
        out_specs=[pl.BlockSpec((tb, vw), fwd(0)), pl.BlockSpec((tb, vw), bwd(0)), state_spec],
        out_shape=[jax.ShapeDtypeStruct((batch * seq, vw), F32),
                   jax.ShapeDtypeStruct((batch * seq, vw), F32),
                   jax.ShapeDtypeStruct((batch, 2, GLA_HEADS, GLA_DV, GLA_DK), F32)],
        compiler_params=_params("arbitrary", "arbitrary"),
        name="gla",
    )(p, p, p, lr, p, p, p, lr, w_up, b_gate, s0)


def _cmlp_kernel(u_ref, v_ref, lng_ref, lnb_ref, ws_ref, bs_ref, o_ref, *, n_chunks):
    gv = _gelu(v_ref[...].astype(F32))
    mu = jnp.mean(gv, axis=-1, keepdims=True)
    dv = gv - mu
    var = jnp.mean(dv * dv, axis=-1, keepdims=True)
    vn = (dv * lax.rsqrt(var + EPS) * lng_ref[...] + lnb_ref[...]).astype(BF16)
    for ch in range(n_chunks):
        rows = slice(ch * CMLP_CHUNK, (ch + 1) * CMLP_CHUNK)
        for g in range(CMLP_GROUPS):
            cols = slice(g * CMLP_GROUP_DIM, (g + 1) * CMLP_GROUP_DIM)
            s = jnp.dot(ws_ref[g], vn[rows, cols], preferred_element_type=F32) + bs_ref[:, g:g + 1]
            o_ref[rows, cols] = (_gelu(u_ref[rows, cols].astype(F32)) * s).astype(o_ref.dtype)


def _outproj_kernel(of_ref, ob_ref, g_ref, u_ref, v_ref, x_ref, mod_ref, gng_ref, n2g_ref,
                    lng_ref, lnb_ref, ws_ref, bs_ref, w_ref, h1_ref, xn2_ref, cm_ref):
    _cmlp_kernel(u_ref, v_ref, lng_ref, lnb_ref, ws_ref, bs_ref, cm_ref,
                 n_chunks=cm_ref.shape[0] // CMLP_CHUNK)
    o = of_ref[...] + ob_ref[...]
    heads = []
    for h in range(GLA_HEADS):
        oh = o[:, h * GLA_DV:(h + 1) * GLA_DV]
        heads.append(oh * lax.rsqrt(jnp.mean(oh * oh, axis=-1, keepdims=True) + EPS))
    on = jnp.concatenate(heads, axis=-1) * gng_ref[...]
    y = (on * _silu(g_ref[...].astype(F32))).astype(BF16)
    mix = (jnp.dot(y, w_ref[0:GLA_WIDTH, :], preferred_element_type=F32)
           + jnp.dot(cm_ref[...], w_ref[GLA_WIDTH:GLA_WIDTH + CMLP_WIDTH, :], preferred_element_type=F32))
    h1 = x_ref[...] + mod_ref[0, 2:3, :] * mix
    h1_ref[...] = h1
    hn = h1 * lax.rsqrt(jnp.mean(h1 * h1, axis=-1, keepdims=True) + EPS) * n2g_ref[...]
    xn2_ref[...] = (hn * (1.0 + mod_ref[0, 4:5, :]) + mod_ref[0, 3:4, :]).astype(xn2_ref.dtype)


def _outproj(o_f, o_b, p, x2, mod3, gla_norm_g, norm2_g, ln_g, ln_b, w_s, b_st, w_out, seq):
    t = x2.shape[0]
    tm = min(512, t)
    g_blk = (2 * GLA_KEY_WIDTH + GLA_WIDTH) // GLA_WIDTH
    cu_blk = (2 * GLA_KEY_WIDTH + 2 * GLA_WIDTH) // CMLP_WIDTH
    return pl.pallas_call(
        _outproj_kernel,
        grid=(t // tm,),
        in_specs=[pl.BlockSpec((tm, GLA_WIDTH), lambda i: (i, 0)),
                  pl.BlockSpec((tm, GLA_WIDTH), lambda i: (i, 0)),
                  pl.BlockSpec((tm, GLA_WIDTH), lambda i: (i, g_blk)),
                  pl.BlockSpec((tm, CMLP_WIDTH), lambda i: (i, cu_blk)),
                  pl.BlockSpec((tm, CMLP_WIDTH), lambda i: (i, cu_blk + 1)),
                  pl.BlockSpec((tm, D_MODEL), lambda i: (i, 0)),
                  pl.BlockSpec((1, N_MOD, D_MODEL), lambda i: ((i * tm) // seq, 0, 0)),
                  pl.BlockSpec((1, GLA_WIDTH), lambda i: (0, 0)),
                  pl.BlockSpec((1, D_MODEL), lambda i: (0, 0)),
                  pl.BlockSpec((1, CMLP_WIDTH), lambda i: (0, 0)),
                  pl.BlockSpec((1, CMLP_WIDTH), lambda i: (0, 0)),
                  pl.BlockSpec((CMLP_GROUPS, CMLP_CHUNK, CMLP_CHUNK), lambda i: (0, 0, 0)),
                  pl.BlockSpec((CMLP_CHUNK, CMLP_GROUPS), lambda i: (0, 0)),
                  pl.BlockSpec((GLA_WIDTH + CMLP_WIDTH, D_MODEL), lambda i: (0, 0))],
        out_specs=[pl.BlockSpec((tm, D_MODEL), lambda i: (i, 0)),
                   pl.BlockSpec((tm, D_MODEL), lambda i: (i, 0))],
        out_shape=[jax.ShapeDtypeStruct((t, D_MODEL), F32),
                   jax.ShapeDtypeStruct((t, D_MODEL), BF16)],
        scratch_shapes=[pltpu.VMEM((tm, CMLP_WIDTH), BF16)],
        compiler_params=_params("arbitrary"),
        name="outproj",
    )(o_f, o_b, p, p, p, x2, mod3, gla_norm_g, norm2_g, ln_g, ln_b, w_s, b_st, w_out)


def _top16_exact(s, key_iota):
    neg = jnp.float32(-jnp.inf)
    vals = []
    rank = jnp.full(s.shape, float(PEER_TOPK), F32)
    for r in range(PEER_TOPK):
        m = jnp.max(s, axis=0, keepdims=True)
        idx = jnp.min(jnp.where(s == m, key_iota, float(s.shape[0])), axis=0, keepdims=True)
        sel = key_iota == idx
        rank = jnp.where(sel, float(r), rank)
        s = jnp.where(sel, neg, s)
        vals.append(m)
    return jnp.concatenate(vals, axis=0), rank


SUBLANES = 8


def _key_groups(s):
    return [s[q * SUBLANES:(q + 1) * SUBLANES] for q in range(s.shape[0] // SUBLANES)]


def _oddeven_mergesort_pairs(n):
    pairs = []

    def merge(lo, hi, r):
        step = r * 2
        if step < hi - lo:
            merge(lo, hi, step)
            merge(lo + r, hi, step)
            pairs.extend((i, i + r) for i in range(lo + r, hi - r, step))
        else:
            pairs.append((lo, lo + r))

    def sort(lo, hi):
        if hi - lo >= 1:
            mid = lo + (hi - lo) // 2
            sort(lo, mid)
            sort(mid + 1, hi)
            merge(lo, hi, 1)

    sort(0, n - 1)
    return pairs


def _sorted_top16(s):
    k = PEER_TOPK
    groups = _key_groups(s)
    assert len(groups) <= k
    rows = groups + [None] * (k - len(groups))

    def vmax(a, b):
        return b if a is None else a if b is None else jnp.maximum(a, b)

    def exchange(i, j):
        if rows[j] is None:
            return
        if rows[i] is None:
            rows[i], rows[j] = rows[j], None
        else:
            rows[i], rows[j] = jnp.maximum(rows[i], rows[j]), jnp.minimum(rows[i], rows[j])

    for i, j in _oddeven_mergesort_pairs(k):
        exchange(i, j)
    shift = SUBLANES // 2
    while shift:
        other = [None if r is None else pltpu.roll(r, shift, 0) for r in rows]
        rows = [vmax(rows[i], other[k - 1 - i]) for i in range(k)]
        d = k // 2
        while d:
            for i in range(k):
                if not i & d:
                    exchange(i, i + d)
            d //= 2
        shift //= 2
    reach = sum(jnp.where(g >= rows[k - 1], 1.0, 0.0) for g in groups)
    repeats = sum(jnp.where(rows[b] == rows[b + 1], 1.0, 0.0) for b in range(k - 1))
    taken = jnp.sum(reach, axis=0, keepdims=True) + float(k) * repeats[0:1]
    return rows, taken


_CAND_INVALID = 1 << 20
_CAND_ROWS = 72


def _staircase_positions():
    assert PEER_TOPK == 16
    k, inv = PEER_TOPK, _CAND_INVALID
    pos = list(range(k))
    pos += [inv] + [a * k for a in range(1, k)]
    pos += [inv] + [k + b for b in range(1, 8)]
    pos += [inv, inv] + [a * k + 1 for a in range(2, 8)]
    pos += [2 * k + b if b in (2, 3, 4) else inv for b in range(8)]
    pos += [3 * k + b if b in (2, 3) else inv for b in range(8)]
    pos += [4 * k + b if b == 2 else inv for b in range(8)]
    assert len(pos) == _CAND_ROWS and sum(p != inv for p in pos) == 50
    return jnp.broadcast_to(jnp.array(pos, F32)[:, None], (_CAND_ROWS, ROUTE_LANES))


ROUTE_LANES = 128
ROUTE_TILES_PER_ITER = 4


def _route_kernel(xn_ref, wqt_ref, sk_ref, pos_ref, e1_ref, c1_ref, e2_ref, r2_ref, qt_ref):
    qt_ref[...] = lax.dot_general(wqt_ref[...], xn_ref[...], _NT, preferred_element_type=F32)
    n = ROUTE_LANES
    n_tiles = xn_ref.shape[0] // n
    k = PEER_TOPK
    key_iota = lax.broadcasted_iota(jnp.int32, (PEER_NKEYS, n), 0).astype(F32)
    row16 = lax.broadcasted_iota(jnp.int32, (k, n), 0)
    neg = jnp.float32(-jnp.inf)
    invalid = float(_CAND_INVALID)

    def tiles(t, carry):
        first = t * ROUTE_TILES_PER_ITER
        taken = [tile(first + u, exact=False) for u in range(ROUTE_TILES_PER_ITER)]

        @pl.when(jnp.max(functools.reduce(jnp.maximum, taken)) > float(k))
        def _():
            for u in range(ROUTE_TILES_PER_ITER):
                tile(first + u, exact=True)
        return carry

    def tile(t, exact):
        h = t // n_tiles
        lanes = pl.ds(pl.multiple_of((t % n_tiles) * n, n), n)
        base = pl.multiple_of(h * PEER_QDIM, PEER_QDIM)
        q1 = qt_ref[pl.ds(base, PEER_HALF), lanes].astype(BF16)
        q2 = qt_ref[pl.ds(base + PEER_HALF, PEER_HALF), lanes].astype(BF16)
        s1 = jnp.dot(sk_ref[0, h], q1, preferred_element_type=F32)
        s2 = jnp.dot(sk_ref[1, h], q2, preferred_element_type=F32)
        if exact:
            v1, rank1 = _top16_exact(s1, key_iota)
            v2, rank2 = _top16_exact(s2, key_iota)
        else:
            rows1, taken1 = _sorted_top16(s1)
            rows2, taken2 = _sorted_top16(s2)
            v1 = jnp.concatenate([r[0:1] for r in rows1], axis=0)
            v2 = jnp.concatenate([r[0:1] for r in rows2], axis=0)
            ranks = []
            for piece in _key_groups(s2):
                rk = jnp.full(piece.shape, float(k), F32)
                for b in range(k):
                    rk = jnp.where(piece == rows2[b], float(b), rk)
                ranks.append(rk)
            rank2 = jnp.concatenate(ranks, axis=0)
        pos = pos_ref[...]
        cand = jnp.concatenate([v1[0:1] + v2, v1 + v2[0:1], v1[1:2] + v2[0:8], v1[0:8] + v2[1:2],
                                v1[2:3] + v2[0:8], v1[3:4] + v2[0:8], v1[4:5] + v2[0:8]], axis=0)
        cand = jnp.where(pos < invalid, cand, neg)
        best = cand[0:1, :]
        if exact:
            rem = cand
            for _ in range(k):
                m = jnp.max(rem, axis=0, keepdims=True)
                idx = jnp.min(jnp.where(rem == m, pos, invalid), axis=0, keepdims=True)
                rem = jnp.where(pos == idx, neg, rem)
            chosen = rem != cand
        else:
            best_pairs, taken_pairs = _sorted_top16(cand)
            chosen = jnp.concatenate([g >= best_pairs[k - 1] for g in _key_groups(cand)], axis=0)
        z = jnp.sum(jnp.where(chosen, jnp.exp(cand - best), 0.0), axis=0, keepdims=True)
        ch = chosen.astype(F32)
        rowsum = lambda lo, hi: jnp.sum(ch[lo:hi], axis=0, keepdims=True)
        cnt = ch[16:32] + jnp.concatenate([ch[40:48], jnp.zeros((8, n), F32)], axis=0)
        for a, (lo, hi) in enumerate(((0, 16), (32, 40), (48, 56), (56, 64), (64, 72))):
            cnt = cnt + jnp.where(row16 == a, rowsum(lo, hi), 0.0)
        c1 = jnp.zeros((PEER_NKEYS, n), F32)
        for a in range(k):
            is_rank_a = (rank1 == float(a)) if exact else (s1 == v1[a:a + 1])
            c1 = jnp.where(is_rank_a, cnt[a:a + 1], c1)
        e1_ref[h, :, lanes] = jnp.exp(s1 - v1[0:1, :]) * (1.0 / z)
        c1_ref[h, :, lanes] = c1
        e2_ref[h, :, lanes] = jnp.exp(s2 - v2[0:1, :]).astype(e2_ref.dtype)
        r2_ref[h, :, lanes] = rank2.astype(r2_ref.dtype)
        if exact:
            return None
        return jnp.maximum(jnp.maximum(taken1, taken2), taken_pairs)

    lax.fori_loop(0, PEER_HEADS * n_tiles // ROUTE_TILES_PER_ITER, tiles, 0)


def _route(xn2, wq_t, sk):
    t = xn2.shape[0]
    tr = min(512, t)
    tab = lambda dt: jax.ShapeDtypeStruct((PEER_HEADS, PEER_NKEYS, t), dt)
    tab_spec = pl.BlockSpec((PEER_HEADS, PEER_NKEYS, tr), lambda i: (0, 0, i))
    return pl.pallas_call(
        _route_kernel,
        grid=(t // tr,),
        in_specs=[pl.BlockSpec((tr, D_MODEL), lambda i: (i, 0)),
                  pl.BlockSpec((PEER_HEADS * PEER_QDIM, D_MODEL), lambda i: (0, 0)),
                  pl.BlockSpec((2, PEER_HEADS, PEER_NKEYS, PEER_HALF), lambda i: (0, 0, 0, 0)),
                  pl.BlockSpec((_CAND_ROWS, ROUTE_LANES), lambda i: (0, 0))],
        out_specs=[tab_spec] * 4,
        out_shape=[tab(F32), tab(F32), tab(BF16), tab(BF16)],
        scratch_shapes=[pltpu.VMEM((PEER_HEADS * PEER_QDIM, tr), F32)],
        compiler_params=_params("arbitrary"),
        name="route",
    )(xn2, wq_t, sk, _staircase_positions())


PEER_CHUNK = 512
PEER_CHUNKS_PER_STEP = 2
PEER_TOKENS = 512
PEER_WEIGHT_LANES = 256
BF16_ROWS = 16
PEER_VMEM_LIMIT_BYTES = 63 * 1024 * 1024


def _peer_kernel(xn_ref, e1_ref, c1_ref, e2_ref, r2_ref, u_ref, vt_ref, h1_ref, mod_ref, fg_ref,
                 o_ref, acc_ref, xs_ref):
    j = pl.program_id(1)
    ec = PEER_CHUNK
    n = xn_ref.shape[0]
    lanes = PEER_WEIGHT_LANES
    rows_per_chunk = ec // PEER_NKEYS
    rows_per_step = PEER_CHUNKS_PER_STEP * rows_per_chunk

    @pl.when(j == 0)
    def _():
        acc_ref[...] = jnp.zeros_like(acc_ref)
        xs_ref[...] = xn_ref[...]

    zero = jnp.zeros((), BF16)
    keys = pl.ds(pl.multiple_of(j * rows_per_step, rows_per_step), rows_per_step)
    chains = [(kk, tc) for kk in range(PEER_CHUNKS_PER_STEP) for tc in range(n // lanes)]

    def routing_weights(kk, tc):
        cols = slice(tc * lanes, (tc + 1) * lanes)
        row = (BF16_ROWS, lanes)
        reps = PEER_NKEYS // BF16_ROWS
        blocks = []
        for r in range(rows_per_chunk):
            i1 = kk * rows_per_chunk + r
            w = None
            for h in range(PEER_HEADS):
                e1 = jnp.broadcast_to(e1_ref[h, keys, cols][i1:i1 + 1], row).astype(BF16)
                c1 = jnp.broadcast_to(c1_ref[h, keys, cols][i1:i1 + 1], row).astype(BF16)
                e1 = jnp.concatenate([e1] * reps, axis=0)
                c1 = jnp.concatenate([c1] * reps, axis=0)
                t = e1 * jnp.where(r2_ref[h, :, cols] < c1, e2_ref[h, :, cols], zero)
                w = t if w is None else w + t
            blocks.append(w)
        return jnp.concatenate(blocks, axis=0)

    acts = [lax.dot_general(u_ref[kk * ec:(kk + 1) * ec, :], xs_ref[tc * lanes:(tc + 1) * lanes, :], _NT,
                            preferred_element_type=F32) for kk, tc in chains]
    for (kk, tc), act in zip(chains, acts):
        p = routing_weights(kk, tc) * _gelu(act).astype(BF16)
        acc_ref[:, tc * lanes:(tc + 1) * lanes] += jnp.dot(vt_ref[:, kk * ec:(kk + 1) * ec], p,
                                                           preferred_element_type=F32)

    @pl.when(j == pl.num_programs(1) - 1)
    def _():
        h2 = h1_ref[...] + mod_ref[0, 5:6, :] * acc_ref[...].T
        o_ref[...] = h2 * lax.rsqrt(jnp.mean(h2 * h2, axis=-1, keepdims=True) + EPS) * fg_ref[...]


def _peer(xn2, tabs, u, v_t, h1, mod3, final_g, seq):
    t = xn2.shape[0]
    tb = min(PEER_TOKENS, t)
    ecs = PEER_CHUNKS_PER_STEP * PEER_CHUNK
    tab_spec = pl.BlockSpec((PEER_HEADS, PEER_NKEYS, tb), lambda i, j: (0, 0, i))
    return pl.pallas_call(
        _peer_kernel,
        grid=(t // tb, PEER_EXPERTS // ecs),
        in_specs=[pl.BlockSpec((tb, D_MODEL), lambda i, j: (i, 0)),
                  tab_spec, tab_spec, tab_spec, tab_spec,
                  pl.BlockSpec((ecs, D_MODEL), lambda i, j: (j, 0)),
                  pl.BlockSpec((D_MODEL, ecs), lambda i, j: (0, j)),
                  pl.BlockSpec((tb, D_MODEL), lambda i, j: (i, 0)),
                  pl.BlockSpec((1, N_MOD, D_MODEL), lambda i, j: ((i * tb) // seq, 0, 0)),
                  pl.BlockSpec((1, D_MODEL), lambda i, j: (0, 0))],
        out_specs=pl.BlockSpec((tb, D_MODEL), lambda i, j: (i, 0)),
        out_shape=jax.ShapeDtypeStruct((t, D_MODEL), F32),
        scratch_shapes=[pltpu.VMEM((D_MODEL, tb), F32), pltpu.VMEM((tb, D_MODEL), BF16)],
        compiler_params=pltpu.CompilerParams(dimension_semantics=("arbitrary", "arbitrary"),
                                             vmem_limit_bytes=PEER_VMEM_LIMIT_BYTES),
        name="peer",
    )(xn2, *tabs, u, v_t, h1, mod3, final_g)


def kernel(x, c, ctx, c_ctx, norm1_g, norm2_g, w_mod, b_mod, w_in, w_gate_up, b_gate, gla_norm_g,
           cmlp_ln_g, cmlp_ln_b, w_spatial, b_spatial, w_out, peer_wq, peer_sub_keys, peer_u, peer_v,
           final_norm_g):
    batch, seq, d = x.shape
    ctx_len = ctx.shape[1]
    depth = w_mod.shape[0]
    assert d == D_MODEL and depth == 1 and batch + 1 <= MOD_ROWS
    assert seq % MAX_TOKEN_TILE == 0 and ctx_len % GLA_CHUNK == 0
    i = 0

    cc = jnp.zeros((MOD_ROWS, d), F32).at[:batch].set(c).at[batch].set(c_ctx)
    mod3 = _modulation(cc, w_mod[i], b_mod[i]).reshape(MOD_ROWS, N_MOD, d)

    qk_end = 2 * GLA_KEY_WIDTH + 2 * GLA_WIDTH
    lr_end = qk_end + 2 * GLA_GATE_RANK
    w = w_in[i].astype(BF16)
    w_main = jnp.concatenate([w[:, :qk_end], w[:, lr_end:]], axis=1)
    w_lr = jnp.pad(w[:, qk_end:lr_end], ((0, 0), (0, LR_PAD - 2 * GLA_GATE_RANK)))
    n1 = norm1_g[i].reshape(1, d)
    x2 = x.reshape(batch * seq, d)
    p_lat, lr_lat = _inproj(x2, mod3, n1, w_main, w_lr, lambda t, tm: (t * tm) // seq)
    p_ctx, lr_ctx = _inproj(ctx.reshape(batch * ctx_len, d), mod3, n1, w_main, w_lr, lambda t, tm: batch)

    w_up = jnp.zeros((2, LR_PAD, GLA_KEY_WIDTH), F32)
    for dd in range(2):
        w_up = w_up.at[dd, dd * GLA_GATE_RANK:(dd + 1) * GLA_GATE_RANK].set(w_gate_up[i, dd])
    w_up = w_up.astype(BF16)
    bg = b_gate[i].reshape(2, 1, GLA_KEY_WIDTH)
    s_zero = jnp.zeros((batch, 2, GLA_HEADS, GLA_DV, GLA_DK), F32)
    _, _, s_ctx = _gla(p_ctx, lr_ctx, w_up, bg, s_zero, batch, ctx_len)
    o_f, o_b, _ = _gla(p_lat, lr_lat, w_up, bg, s_ctx, batch, seq)

    h1, xn2 = _outproj(o_f, o_b, p_lat, x2, mod3, gla_norm_g[i].reshape(1, -1), norm2_g[i].reshape(1, d),
                       cmlp_ln_g[i].reshape(1, -1), cmlp_ln_b[i].reshape(1, -1),
                       w_spatial[i].astype(BF16), b_spatial[i].T, w_out[i].astype(BF16), seq)

    tabs = _route(xn2, peer_wq[i].T.astype(BF16), peer_sub_keys[i].astype(BF16))
    out = _peer(xn2, tabs, peer_u[i].astype(BF16), peer_v[i].T.astype(BF16), h1, mod3,
                final_norm_g.reshape(1, d), seq)
    return out.reshape(batch, seq, d)
```
